```python
import math
import jax, jax.numpy as jnp
from jax import lax
import numpy as np


D_MODEL = 2048
BATCH = 1
SEQ = 8192
DEPTH = 2
DEC_BATCH = 128
DEC_SEQ = 8
PAST_LEN = 2048
PAGE_SIZE = 128

N_A_LAYERS = DEPTH // 2
N_B_LAYERS = DEPTH - N_A_LAYERS
SSM_GROUP = 16
SSM_GROUPS = D_MODEL // SSM_GROUP
SSM_STATE = 64
HEAD_DIM = 128
N_HEADS = D_MODEL // HEAD_DIM
N_KV_HEADS = 4
Q_PER_KV = N_HEADS // N_KV_HEADS
D_FF = 4 * D_MODEL
PLE_DIM = 256
BLOCK_Q = 128
LN_EPS = 1e-5
DN_ALPHA = (2.0 * DEPTH) ** 0.25
DN_BETA = (8.0 * DEPTH) ** -0.25
DT_MIN = 0.001
DT_MAX = 0.1
SB_BIAS_INIT = -6.0

kernel_name = 'yoco_s5_stickbreaking_decoder_step'


def layer_norm(x, g, b):
    xf = x.astype(jnp.float32)
    mu = jnp.mean(xf, axis=-1, keepdims=True)
    var = jnp.mean(jnp.square(xf - mu), axis=-1, keepdims=True)
    return ((xf - mu) * lax.rsqrt(var + LN_EPS) * g.astype(jnp.float32) + b.astype(jnp.float32)).astype(x.dtype)


def post_norm(x, sub, g, b):
    return layer_norm(DN_ALPHA * x + sub.astype(x.dtype), g, b)


def _ssm_combine(e1, e2):
    a1r, a1i, b1r, b1i = e1
    a2r, a2i, b2r, b2i = e2
    return (a1r * a2r - a1i * a2i,
            a1r * a2i + a1i * a2r,
            a2r * b1r - a2i * b1i + b2r,
            a2r * b1i + a2i * b1r + b2i)


def s5_mixer(x, h0_re, h0_im, lam_re, lam_im, log_dt, b_re, b_im, c_re, c_im, d_skip, w_gate, w_out):
    bsz, t, _ = x.shape
    f32 = jnp.float32
    lam_re = lam_re.astype(f32)
    lam_im = lam_im.astype(f32)
    dt = jnp.exp(log_dt.astype(f32))[:, None]
    mag = jnp.exp(lam_re * dt)
    ang = lam_im * dt
    ab_re = mag * jnp.cos(ang)
    ab_im = mag * jnp.sin(ang)
    den = lam_re * lam_re + lam_im * lam_im
    nr = ab_re - 1.0
    f_re = (nr * lam_re + ab_im * lam_im) / den
    f_im = (ab_im * lam_re - nr * lam_im) / den
    b_re = b_re.astype(f32)
    b_im = b_im.astype(f32)
    bb_re = f_re[..., None] * b_re - f_im[..., None] * b_im
    bb_im = f_re[..., None] * b_im + f_im[..., None] * b_re
    u = x.astype(f32).reshape(bsz, t, SSM_GROUPS, SSM_GROUP)
    bu_re = jnp.einsum('btgc,gpc->btgp', u, bb_re)
    bu_im = jnp.einsum('btgc,gpc->btgp', u, bb_im)
    a_re = jnp.broadcast_to(ab_re, bu_re.shape)
    a_im = jnp.broadcast_to(ab_im, bu_im.shape)
    acum_re, acum_im, h_re, h_im = lax.associative_scan(
        _ssm_combine, (a_re, a_im, bu_re, bu_im), axis=1)
    h0r = h0_re.astype(f32)[:, None]
    h0i = h0_im.astype(f32)[:, None]
    h_re, h_im = (h_re + acum_re * h0r - acum_im * h0i,
                  h_im + acum_re * h0i + acum_im * h0r)
    y = (jnp.einsum('btgp,gcp->btgc', h_re, c_re.astype(f32))
         - jnp.einsum('btgp,gcp->btgc', h_im, c_im.astype(f32)))
    y = y.reshape(bsz, t, D_MODEL) + d_skip.astype(f32) * x.astype(f32)
    g = jax.nn.gelu(y).astype(x.dtype)
    out = (g * jax.nn.sigmoid(g @ w_gate)) @ w_out
    return out, h_re[:, -1], h_im[:, -1]


def _sb_block(q, k, v, q_pos, k_pos, bias):
    bsz, qb, _, _ = q.shape
    f32 = jnp.float32
    qg = q.astype(f32).reshape(bsz, qb, N_KV_HEADS, Q_PER_KV, HEAD_DIM)
    z = jnp.einsum('bqhgd,bkhd->bhgqk', qg, k.astype(f32)) * (HEAD_DIM ** -0.5)
    z = z + bias.astype(f32).reshape(N_KV_HEADS, Q_PER_KV)[None, :, :, None, None]
    causal = k_pos[None, :] < q_pos[:, None]
    log_stay = jnp.where(causal, jax.nn.log_sigmoid(-z), 0.0)
    log_w = jax.nn.log_sigmoid(z) + lax.cumsum(log_stay, axis=4, reverse=True) - log_stay
    w = jnp.where(causal, jnp.exp(log_w), 0.0)
    o = jnp.einsum('bhgqk,bkhd->bqhgd', w, v.astype(f32))
    return o.reshape(bsz, qb, N_HEADS, HEAD_DIM).astype(v.dtype)


def stick_breaking_attention(q, k, v, q_pos, k_pos, bias):
    bsz, t, h, d = q.shape
    qb = BLOCK_Q if t % BLOCK_Q == 0 else t
    nb = t // qb
    q_blocks = q.reshape(bsz, nb, qb, h, d).transpose(1, 0, 2, 3, 4)
    pos_blocks = q_pos.reshape(nb, qb)
    out = lax.map(lambda a: _sb_block(a[0], k, v, a[1], k_pos, bias), (q_blocks, pos_blocks))
    return out.transpose(1, 0, 2, 3, 4).reshape(bsz, t, h, d)


def squared_relu_mlp(x, w1, w2):
    return jnp.square(jax.nn.relu(x @ w1)) @ w2


def per_layer_embedding(x, p, w_proj, w_gate):
    return (p.astype(x.dtype) @ w_proj) * jax.nn.sigmoid(x @ w_gate)


def run_trunk(x, p, h0_re, h0_im, k_past, v_past, pos0, P):
    bsz, t, _ = x.shape
    q_pos = pos0 + jnp.arange(t)
    k_pos = jnp.arange(pos0 + t)
    new_re, new_im = [], []
    k_new = v_new = k_all = v_all = None
    for i in range(DEPTH):
        if i < N_A_LAYERS:
            mix, hr, hi = s5_mixer(x, h0_re[i], h0_im[i], P['s5_lam_re'][i], P['s5_lam_im'][i],
                                   P['s5_log_dt'][i], P['s5_b_re'][i], P['s5_b_im'][i],
                                   P['s5_c_re'][i], P['s5_c_im'][i], P['s5_d'][i],
                                   P['s5_w_gate'][i], P['s5_w_out'][i])
            new_re.append(hr)
            new_im.append(hi)
        else:
            if i == N_A_LAYERS:
                k_new = (x @ P['w_k']).reshape(bsz, t, N_KV_HEADS, HEAD_DIM)
                v_new = (x @ P['w_v']).reshape(bsz, t, N_KV_HEADS, HEAD_DIM)
                k_all = jnp.concatenate([k_past.astype(k_new.dtype), k_new], axis=1)
                v_all = jnp.concatenate([v_past.astype(v_new.dtype), v_new], axis=1)
            j = i - N_A_LAYERS
            q = (x @ P['sb_w_q'][j]).reshape(bsz, t, N_HEADS, HEAD_DIM)
            o = stick_breaking_attention(q, k_all, v_all, q_pos, k_pos, P['sb_bias'][j])
            mix = o.reshape(bsz, t, N_HEADS * HEAD_DIM) @ P['sb_w_o'][j]
        x = post_norm(x, mix, P['ln_mix_g'][i], P['ln_mix_b'][i])
        x = post_norm(x, squared_relu_mlp(x, P['mlp_w1'][i], P['mlp_w2'][i]),
                      P['ln_mlp_g'][i], P['ln_mlp_b'][i])
        x = post_norm(x, per_layer_embedding(x, p[i], P['ple_w_proj'][i], P['ple_w_gate'][i]),
                      P['ln_ple_g'][i], P['ln_ple_b'][i])
    return x, jnp.stack(new_re), jnp.stack(new_im), k_new, v_new


def setup_inputs(seed: int = 0) -> dict:
    key = jax.random.key(seed)
    ks = jax.random.split(key, 40)
    f32 = jnp.float32

    def dense(k, shape, fan_in, scale=1.0):
        return jax.random.normal(k, shape, f32) * (scale * fan_in ** -0.5)

    n_pages = PAST_LEN // PAGE_SIZE
    n_used = DEC_BATCH * n_pages
    n_phys = (5 * n_used + 3) // 4
    page_table = jax.random.permutation(ks[8], n_phys)[:n_used].reshape(DEC_BATCH, n_pages).astype(jnp.int32)
    state_shape = (N_A_LAYERS, DEC_BATCH, SSM_GROUPS, SSM_STATE)
    gp = (N_A_LAYERS, SSM_GROUPS, SSM_STATE)
    return {
        'x_prompt': jax.random.normal(ks[0], (BATCH, SEQ, D_MODEL), f32),
        'x_sample': jax.random.normal(ks[1], (DEC_BATCH, DEC_SEQ, D_MODEL), f32),
        'p_prompt': jax.random.normal(ks[2], (DEPTH, BATCH, SEQ, PLE_DIM), f32),
        'p_sample': jax.random.normal(ks[3], (DEPTH, DEC_BATCH, DEC_SEQ, PLE_DIM), f32),
        'state_ssm_re': 0.1 * jax.random.normal(ks[4], state_shape, f32),
        'state_ssm_im': 0.1 * jax.random.normal(ks[5], state_shape, f32),
        'cache_k': jax.random.normal(ks[6], (n_phys, PAGE_SIZE, N_KV_HEADS, HEAD_DIM), f32),
        'cache_v': DN_BETA * jax.random.normal(ks[7], (n_phys, PAGE_SIZE, N_KV_HEADS, HEAD_DIM), f32),
        'page_table': page_table,
        's5_lam_re': -0.5 + 0.01 * jax.random.normal(ks[9], gp, f32),
        's5_lam_im': math.pi * jnp.arange(SSM_STATE, dtype=f32) + 0.01 * jax.random.normal(ks[10], gp, f32),
        's5_log_dt': jax.random.uniform(ks[11], (N_A_LAYERS, SSM_GROUPS), f32,
                                        math.log(DT_MIN), math.log(DT_MAX)),
        's5_b_re': dense(ks[12], (N_A_LAYERS, SSM_GROUPS, SSM_STATE, SSM_GROUP), 2 * SSM_GROUP),
        's5_b_im': dense(ks[13], (N_A_LAYERS, SSM_GROUPS, SSM_STATE, SSM_GROUP), 2 * SSM_GROUP),
        's5_c_re': dense(ks[14], (N_A_LAYERS, SSM_GROUPS, SSM_GROUP, SSM_STATE), SSM_STATE),
        's5_c_im': dense(ks[15], (N_A_LAYERS, SSM_GROUPS, SSM_GROUP, SSM_STATE), SSM_STATE),
        's5_d': jax.random.normal(ks[16], (N_A_LAYERS, D_MODEL), f32),
        's5_w_gate': dense(ks[17], (N_A_LAYERS, D_MODEL, D_MODEL), D_MODEL),
        's5_w_out': dense(ks[18], (N_A_LAYERS, D_MODEL, D_MODEL), D_MODEL, DN_BETA),
        'w_k': dense(ks[19], (D_MODEL, N_KV_HEADS * HEAD_DIM), D_MODEL),
        'w_v': dense(ks[20], (D_MODEL, N_KV_HEADS * HEAD_DIM), D_MODEL, DN_BETA),
        'sb_w_q': dense(ks[21], (N_B_LAYERS, D_MODEL, N_HEADS * HEAD_DIM), D_MODEL),
        'sb_w_o': dense(ks[22], (N_B_LAYERS, N_HEADS * HEAD_DIM, D_MODEL), N_HEADS * HEAD_DIM, DN_BETA),
        'sb_bias': SB_BIAS_INIT + 0.1 * jax.random.normal(ks[33], (N_B_LAYERS, N_HEADS), f32),
        'ln_mix_g': 1.0 + 0.02 * jax.random.normal(ks[23], (DEPTH, D_MODEL), f32),
        'ln_mix_b': 0.02 * jax.random.normal(ks[24], (DEPTH, D_MODEL), f32),
        'mlp_w1': dense(ks[25], (DEPTH, D_MODEL, D_FF), D_MODEL),
        'mlp_w2': dense(ks[26], (DEPTH, D_FF, D_MODEL), D_FF, DN_BETA),
        'ln_mlp_g': 1.0 + 0.02 * jax.random.normal(ks[27], (DEPTH, D_MODEL), f32),
        'ln_mlp_b': 0.02 * jax.random.normal(ks[28], (DEPTH, D_MODEL), f32),
        'ple_w_proj': dense(ks[29], (DEPTH, PLE_DIM, D_MODEL), PLE_DIM, DN_BETA),
        'ple_w_gate': dense(ks[30], (DEPTH, D_MODEL, D_MODEL), D_MODEL),
        'ln_ple_g': 1.0 + 0.02 * jax.random.normal(ks[31], (DEPTH, D_MODEL), f32),
        'ln_ple_b': 0.02 * jax.random.normal(ks[32], (DEPTH, D_MODEL), f32),
    }


def reference(x_prompt, x_sample, p_prompt, p_sample, state_ssm_re, state_ssm_im, cache_k, cache_v,
              page_table, s5_lam_re, s5_lam_im, s5_log_dt, s5_b_re, s5_b_im, s5_c_re, s5_c_im, s5_d,
              s5_w_gate, s5_w_out, w_k, w_v, sb_w_q, sb_w_o, sb_bias, ln_mix_g, ln_mix_b, mlp_w1, mlp_w2,
              ln_mlp_g, ln_mlp_b, ple_w_proj, ple_w_gate, ln_ple_g, ln_ple_b):
    params = dict(s5_lam_re=s5_lam_re, s5_lam_im=s5_lam_im, s5_log_dt=s5_log_dt, s5_b_re=s5_b_re,
                  s5_b_im=s5_b_im, s5_c_re=s5_c_re, s5_c_im=s5_c_im, s5_d=s5_d, s5_w_gate=s5_w_gate,
                  s5_w_out=s5_w_out, w_k=w_k, w_v=w_v, sb_w_q=sb_w_q, sb_w_o=sb_w_o, sb_bias=sb_bias,
                  ln_mix_g=ln_mix_g, ln_mix_b=ln_mix_b, mlp_w1=mlp_w1, mlp_w2=mlp_w2,
                  ln_mlp_g=ln_mlp_g, ln_mlp_b=ln_mlp_b, ple_w_proj=ple_w_proj, ple_w_gate=ple_w_gate,
                  ln_ple_g=ln_ple_g, ln_ple_b=ln_ple_b)
    n_dec, n_pages = page_table.shape
    past_len = n_pages * cache_k.shape[1]
    k_past = cache_k[page_table].reshape(n_dec, past_len, N_KV_HEADS, HEAD_DIM)
    v_past = cache_v[page_table].reshape(n_dec, past_len, N_KV_HEADS, HEAD_DIM)
    bsz = x_prompt.shape[0]
    h0 = jnp.zeros((N_A_LAYERS, bsz, SSM_GROUPS, SSM_STATE), jnp.float32)
    empty = jnp.zeros((bsz, 0, N_KV_HEADS, HEAD_DIM), x_prompt.dtype)
    y_prompt, sre_p, sim_p, k_p, v_p = run_trunk(x_prompt, p_prompt, h0, h0, empty, empty, 0, params)
    y_sample, sre_s, sim_s, k_s, v_s = run_trunk(x_sample, p_sample, state_ssm_re, state_ssm_im,
                                                 k_past, v_past, past_len, params)
    return (y_prompt, y_sample, sre_p, sim_p, k_p, v_p, sre_s, sim_s, k_s, v_s)
```

```python
import functools
import math

import jax
import jax.numpy as jnp
from jax import lax
from jax.experimental import pallas as pl
from jax.experimental.pallas import tpu as pltpu

F32 = jnp.float32
BF16 = jnp.bfloat16

SSM_GROUP = 16
HEAD_DIM = 128
N_KV_HEADS = 4
LN_EPS = 1e-5
DEPTH = 2
DN_ALPHA = (2.0 * DEPTH) ** 0.25

SUBLANES = 8
LANES = 128
CH_BLOCK = 256
VMEM_LIMIT = 56 * 1024 * 1024


def _cparams(*sem):
    return pltpu.CompilerParams(dimension_semantics=sem, vmem_limit_bytes=VMEM_LIMIT)


def _dot(a, b):
    return jnp.dot(a, b, preferred_element_type=F32)


def _sigmoid(x):
    return 1.0 / (1.0 + jnp.exp(-x))


def _layer_norm_rows(y, g, b):
    mu = jnp.mean(y, axis=-1, keepdims=True)
    yc = y - mu
    var = jnp.mean(yc * yc, axis=-1, keepdims=True)
    return yc * lax.rsqrt(var + LN_EPS) * g + b


def _discretize(lam_re, lam_im, dt):
    mag = jnp.exp(lam_re * dt)
    ang = lam_im * dt
    ab_re = mag * jnp.cos(ang)
    ab_im = mag * jnp.sin(ang)
    den = lam_re * lam_re + lam_im * lam_im
    nr = ab_re - 1.0
    f_re = (nr * lam_re + ab_im * lam_im) / den
    f_im = (ab_im * lam_re - nr * lam_im) / den
    return ab_re, ab_im, f_re, f_im


def _s5_prep_kernel(lre_ref, lim_ref, ldt_ref, lre_rep_ref, lim_rep_ref, bre_ref, bim_ref,
                    abre_ref, abim_ref, bbre_ref, bbim_ref):
    dt = jnp.exp(ldt_ref[...])
    ab_re, ab_im, _, _ = _discretize(lre_ref[...], lim_ref[...], dt)
    abre_ref[...] = ab_re
    abim_ref[...] = ab_im
    _, _, f_re, f_im = _discretize(lre_rep_ref[...], lim_rep_ref[...], dt)
    b_re = bre_ref[...]
    b_im = bim_ref[...]
    bbre_ref[...] = f_re * b_re - f_im * b_im
    bbim_ref[...] = f_re * b_im + f_im * b_re


def _s5_prep(lam_re, lam_im, log_dt, b_re, b_im):
    g, p = lam_re.shape
    rep = lambda a: jnp.repeat(a, SSM_GROUP, axis=1)
    out = [jax.ShapeDtypeStruct((g, p), F32)] * 2 + [jax.ShapeDtypeStruct((g, p * SSM_GROUP), F32)] * 2
    return pl.pallas_call(_s5_prep_kernel, out_shape=out, name="s5_prep")(
        lam_re, lam_im, log_dt.reshape(g, 1), rep(lam_re), rep(lam_im),
        b_re.reshape(g, p * SSM_GROUP), b_im.reshape(g, p * SSM_GROUP))


def _block_diag_in(bb, n_blk, gpb, p):
    v = bb.reshape(n_blk, gpb, p, SSM_GROUP).transpose(0, 1, 3, 2)
    eye = jnp.eye(gpb, dtype=bb.dtype)
    w = v[:, :, :, None, :] * eye[None, :, None, :, None]
    return w.reshape(n_blk, gpb * SSM_GROUP, gpb * p)


def _block_diag_out(c, n_blk, gpb, p):
    v = c.reshape(n_blk, gpb, SSM_GROUP, p).transpose(0, 1, 3, 2)
    eye = jnp.eye(gpb, dtype=c.dtype)
    w = v[:, :, :, None, :] * eye[None, :, None, :, None]
    return w.reshape(n_blk, gpb * p, gpb * SSM_GROUP)


def _complex_step(a_re, a_im, h_re, h_im, b_re, b_im):
    return (a_re * h_re - a_im * h_im + b_re, a_re * h_im + a_im * h_re + b_im)


def _scan_prompt_kernel(x_ref, wb_ref, wcre_ref, wcim_ref, are_ref, aim_ref, d_ref,
                        g_ref, hre_ref, him_ref, s_ref, h_ref, *, n_blk, tb, ns):
    nc = ns // LANES

    @pl.when(pl.program_id(0) == 0)
    def _():
        h_ref[...] = jnp.zeros_like(h_ref)

    x = x_ref[...]
    for j in range(n_blk):
        u = x[:, j * CH_BLOCK:(j + 1) * CH_BLOCK].astype(BF16)
        bu = _dot(u, wb_ref[j])
        for c in range(2 * nc):
            s_ref[c, j * tb:(j + 1) * tb, :] = bu[:, c * LANES:(c + 1) * LANES]
    a_re = are_ref[...]
    a_im = aim_ref[...]

    def body(t, carry):
        rows = pl.ds(t, n_blk, stride=tb)
        b_re = jnp.concatenate([s_ref[c, rows, :] for c in range(nc)], axis=1)
        b_im = jnp.concatenate([s_ref[nc + c, rows, :] for c in range(nc)], axis=1)
        h_re, h_im = _complex_step(a_re, a_im, carry[0], carry[1], b_re, b_im)
        for c in range(nc):
            s_ref[c, rows, :] = h_re[:, c * LANES:(c + 1) * LANES]
            s_ref[nc + c, rows, :] = h_im[:, c * LANES:(c + 1) * LANES]
        return h_re, h_im

    h_re, h_im = lax.fori_loop(0, tb, body, (h_ref[:, :ns], h_ref[:, ns:]), unroll=8)
    h_ref[...] = jnp.concatenate([h_re, h_im], axis=1)
    hre_ref[...] = h_re
    him_ref[...] = h_im

    ys = []
    for j in range(n_blk):
        rows = slice(j * tb, (j + 1) * tb)
        hj_re = jnp.concatenate([s_ref[c, rows, :] for c in range(nc)], axis=1)
        hj_im = jnp.concatenate([s_ref[nc + c, rows, :] for c in range(nc)], axis=1)
        ys.append(_dot(hj_re.astype(BF16), wcre_ref[j]) - _dot(hj_im.astype(BF16), wcim_ref[j]))
    y = jnp.concatenate(ys, axis=1) + d_ref[...] * x
    g_ref[...] = jax.nn.gelu(y).astype(g_ref.dtype)


def _scan_prompt(x, wb, wc_re, wc_im, a_re, a_im, d, *, tb):
    t, dm = x.shape
    n_blk, _, ns2 = wb.shape
    ns = ns2 // 2
    assert n_blk == SUBLANES and t % tb == 0
    full = lambda shape: pl.BlockSpec(shape, lambda i: (0,) * len(shape))
    return pl.pallas_call(
        functools.partial(_scan_prompt_kernel, n_blk=n_blk, tb=tb, ns=ns),
        grid=(t // tb,),
        in_specs=[pl.BlockSpec((tb, dm), lambda i: (i, 0)), full(wb.shape), full(wc_re.shape),
                  full(wc_im.shape), full(a_re.shape), full(a_im.shape), full(d.shape)],
        out_specs=[pl.BlockSpec((tb, dm), lambda i: (i, 0)), full((n_blk, ns)), full((n_blk, ns))],
        out_shape=[jax.ShapeDtypeStruct((t, dm), BF16), jax.ShapeDtypeStruct((n_blk, ns), F32),
                   jax.ShapeDtypeStruct((n_blk, ns), F32)],
        scratch_shapes=[pltpu.VMEM((ns2 // LANES, n_blk * tb, LANES), F32), pltpu.VMEM((n_blk, ns2), F32)],
        compiler_params=_cparams("arbitrary"),
        name="s5_scan_prompt",
    )(x, wb, wc_re, wc_im, a_re, a_im, d)


def _scan_sample_kernel(x_ref, wb_ref, wcre_ref, wcim_ref, are_ref, aim_ref, d_ref, h0re_ref, h0im_ref,
                        g_ref, hre_ref, him_ref, s_ref, *, n_grp, n_step, ns):
    x = x_ref[...]
    s_ref[...] = _dot(x.astype(BF16), wb_ref[0])
    a_re = jnp.broadcast_to(are_ref[0], (SUBLANES, ns))
    a_im = jnp.broadcast_to(aim_ref[0], (SUBLANES, ns))

    def body(bg, _):
        b0 = pl.multiple_of(bg * SUBLANES, SUBLANES)
        h_re = h0re_ref[pl.ds(b0, SUBLANES), :]
        h_im = h0im_ref[pl.ds(b0, SUBLANES), :]
        for t in range(n_step):
            r0 = pl.multiple_of((bg * n_step + t) * SUBLANES, SUBLANES)
            tile = s_ref[pl.ds(r0, SUBLANES), :]
            h_re, h_im = _complex_step(a_re, a_im, h_re, h_im, tile[:, :ns], tile[:, ns:])
            s_ref[pl.ds(r0, SUBLANES), :] = jnp.concatenate([h_re, h_im], axis=1)
        hre_ref[pl.ds(b0, SUBLANES), :] = h_re
        him_ref[pl.ds(b0, SUBLANES), :] = h_im
        return 0

    lax.fori_loop(0, n_grp, body, 0)
    h = s_ref[...]
    y = _dot(h[:, :ns].astype(BF16), wcre_ref[0]) - _dot(h[:, ns:].astype(BF16), wcim_ref[0])
    g_ref[...] = jax.nn.gelu(y + d_ref[...] * x).astype(g_ref.dtype)


def _scan_sample(x, wb, wc_re, wc_im, a_re, a_im, d, h0_re, h0_im, *, n_step):
    rows, dm = x.shape
    n_blk, _, ns2 = wb.shape
    ns = ns2 // 2
    n_seq = h0_re.shape[0]
    assert rows == n_seq * n_step and n_seq % SUBLANES == 0
    col = lambda shape: pl.BlockSpec(shape, lambda j: (0, j))
    lead = lambda shape: pl.BlockSpec(shape, lambda j: (j,) + (0,) * (len(shape) - 1))
    return pl.pallas_call(
        functools.partial(_scan_sample_kernel, n_grp=n_seq // SUBLANES, n_step=n_step, ns=ns),
        grid=(n_blk,),
        in_specs=[col((rows, CH_BLOCK)), lead((1, CH_BLOCK, ns2)), lead((1, ns, CH_BLOCK)),
                  lead((1, ns, CH_BLOCK)), lead((1, 1, ns)), lead((1, 1, ns)), col((1, CH_BLOCK)),
                  col((n_seq, ns)), col((n_seq, ns))],
        out_specs=[col((rows, CH_BLOCK)), col((n_seq, ns)), col((n_seq, ns))],
        out_shape=[jax.ShapeDtypeStruct((rows, dm), BF16), jax.ShapeDtypeStruct((n_seq, n_blk * ns), F32),
                   jax.ShapeDtypeStruct((n_seq, n_blk * ns), F32)],
        scratch_shapes=[pltpu.VMEM((rows, ns2), F32)],
        compiler_params=_cparams("arbitrary"),
        name="s5_scan_sample",
    )(x, wb, wc_re, wc_im, a_re.reshape(n_blk, 1, ns), a_im.reshape(n_blk, 1, ns), d, h0_re, h0_im)


def _post_norm_store(o_ref, x, sub, g_ref, b_ref):
    o_ref[...] = _layer_norm_rows(DN_ALPHA * x + sub, g_ref[...], b_ref[...])


def _glu_kernel(a_ref, ablk_ref, x_ref, w1_ref, w2_ref, g_ref, b_ref, o_ref, acc_ref):
    f = pl.program_id(1)

    @pl.when(f == 0)
    def _():
        acc_ref[...] = jnp.zeros_like(acc_ref)

    h = ablk_ref[...].astype(F32) * _sigmoid(_dot(a_ref[...], w1_ref[...]))
    acc_ref[...] += _dot(h.astype(BF16), w2_ref[...])

    @pl.when(f == pl.num_programs(1) - 1)
    def _():
        _post_norm_store(o_ref, x_ref[...], acc_ref[...], g_ref, b_ref)


def _mlp_kernel(x_ref, w1_ref, w2_ref, g_ref, b_ref, o_ref, acc_ref):
    f = pl.program_id(1)

    @pl.when(f == 0)
    def _():
        acc_ref[...] = jnp.zeros_like(acc_ref)

    h = jnp.square(jnp.maximum(_dot(x_ref[...].astype(BF16), w1_ref[...]), 0.0))
    acc_ref[...] += _dot(h.astype(BF16), w2_ref[...])

    @pl.when(f == pl.num_programs(1) - 1)
    def _():
        _post_norm_store(o_ref, x_ref[...], acc_ref[...], g_ref, b_ref)


def _two_matmul_specs(tm, dm, tf):
    row = pl.BlockSpec((tm, dm), lambda i, f: (i, 0))
    w1 = pl.BlockSpec((dm, tf), lambda i, f: (0, f))
    w2 = pl.BlockSpec((tf, dm), lambda i, f: (f, 0))
    vec = pl.BlockSpec((1, dm), lambda i, f: (0, 0))
    return row, w1, w2, vec


def _glu(a, x, w1, w2, g, b, *, tm, tf):
    t, dm = x.shape
    row, w1s, w2s, vec = _two_matmul_specs(tm, dm, tf)
    return pl.pallas_call(
        _glu_kernel, grid=(t // tm, w1.shape[1] // tf),
        in_specs=[row, pl.BlockSpec((tm, tf), lambda i, f: (i, f)), row, w1s, w2s, vec, vec],
        out_specs=row, out_shape=jax.ShapeDtypeStruct((t, dm), F32),
        scratch_shapes=[pltpu.VMEM((tm, dm), F32)],
        compiler_params=_cparams("parallel", "arbitrary"), name="glu",
    )(a, a, x, w1, w2, g, b)


def _mlp(x, w1, w2, g, b, *, tm, tf):
    t, dm = x.shape
    row, w1s, w2s, vec = _two_matmul_specs(tm, dm, tf)
    return pl.pallas_call(
        _mlp_kernel, grid=(t // tm, w1.shape[1] // tf),
        in_specs=[row, w1s, w2s, vec, vec],
        out_specs=row, out_shape=jax.ShapeDtypeStruct((t, dm), F32),
        scratch_shapes=[pltpu.VMEM((tm, dm), F32)],
        compiler_params=_cparams("parallel", "arbitrary"), name="mlp",
    )(x, w1, w2, g, b)


def _ple_kernel(x_ref, p_ref, wp_ref, wg_ref, g_ref, b_ref, o_ref):
    x = x_ref[...]
    sub = _dot(p_ref[...].astype(BF16), wp_ref[...]) * _sigmoid(_dot(x.astype(BF16), wg_ref[...]))
    _post_norm_store(o_ref, x, sub, g_ref, b_ref)


def _ple(x, p, wp, wg, g, b, *, tm):
    t, dm = x.shape
    row = pl.BlockSpec((tm, dm), lambda i: (i, 0))
    full = lambda a: pl.BlockSpec(a.shape, lambda i: (0, 0))
    return pl.pallas_call(
        _ple_kernel, grid=(t // tm,),
        in_specs=[row, pl.BlockSpec((tm, p.shape[1]), lambda i: (i, 0)), full(wp), full(wg), full(g), full(b)],
        out_specs=row, out_shape=jax.ShapeDtypeStruct((t, dm), F32),
        compiler_params=_cparams("parallel"), name="ple",
    )(x, p, wp, wg, g, b)


def _attn_out_kernel(o_ref, x_ref, w_ref, g_ref, b_ref, y_ref):
    _post_norm_store(y_ref, x_ref[...], _dot(o_ref[...], w_ref[...]), g_ref, b_ref)


def _attn_out(o, x, w, g, b, *, tm):
    t, dm = x.shape
    row = pl.BlockSpec((tm, dm), lambda i: (i, 0))
    full = lambda a: pl.BlockSpec(a.shape, lambda i: (0, 0))
    return pl.pallas_call(
        _attn_out_kernel, grid=(t // tm,),
        in_specs=[row, row, full(w), full(g), full(b)],
        out_specs=row, out_shape=jax.ShapeDtypeStruct((t, dm), F32),
        compiler_params=_cparams("parallel"), name="attn_out",
    )(o, x, w, g, b)


def _qkv_kernel(x_ref, wq_ref, wk_ref, wv_ref, wkt_ref, q_ref, k_ref, v_ref, kt_ref, vb_ref):
    xb = x_ref[...].astype(BF16)
    q_ref[...] = (_dot(xb, wq_ref[...]) * (HEAD_DIM ** -0.5)).astype(q_ref.dtype)
    k_ref[...] = _dot(xb, wk_ref[...])
    v = _dot(xb, wv_ref[...])
    v_ref[...] = v
    vb_ref[...] = v.astype(vb_ref.dtype)
    kt = lax.dot_general(wkt_ref[...], xb, (((1,), (1,)), ((), ())), preferred_element_type=F32)
    kt_ref[0] = kt.astype(kt_ref.dtype)


def _qkv(x, wq, wk, wv, wkt, *, tm):
    t, dm = x.shape
    dq, dkv = wq.shape[1], wk.shape[1]
    row = lambda n: pl.BlockSpec((tm, n), lambda i: (i, 0))
    full = lambda a: pl.BlockSpec(a.shape, lambda i: (0, 0))
    return pl.pallas_call(
        _qkv_kernel, grid=(t // tm,),
        in_specs=[row(dm), full(wq), full(wk), full(wv), full(wkt)],
        out_specs=[row(dq), row(dkv), row(dkv), pl.BlockSpec((1, dkv, tm), lambda i: (i, 0, 0)), row(dkv)],
        out_shape=[jax.ShapeDtypeStruct((t, dq), BF16), jax.ShapeDtypeStruct((t, dkv), F32),
                   jax.ShapeDtypeStruct((t, dkv), F32), jax.ShapeDtypeStruct((t // tm, dkv, tm), BF16),
                   jax.ShapeDtypeStruct((t, dkv), BF16)],
        compiler_params=_cparams("parallel"), name="qkv",
    )(x, wq, wk, wv, wkt)


def _softplus(z):
    return jnp.maximum(z, 0.0) + jnp.log(1.0 + jnp.exp(-jnp.abs(z)))


def _suffix_ones(n):
    return (lax.broadcasted_iota(jnp.int32, (n, n), 0) >= lax.broadcasted_iota(jnp.int32, (n, n), 1)).astype(BF16)


def _sb_block(s, bias, v, u, carry, mask):
    z = s + bias
    sp = _softplus(z)
    if mask is not None:
        sp = jnp.where(mask, sp, 0.0)
    hi = sp.astype(BF16)
    lo = (sp - hi.astype(F32)).astype(BF16)
    c = _dot(hi, u) + _dot(lo, u) + carry
    w = jnp.exp(z - c)
    if mask is not None:
        w = jnp.where(mask, w, 0.0)
    return _dot(w.astype(BF16), v), c[:, :1]


def _attn_prompt_kernel(bias_ref, q_ref, kt_ref, v_ref, o_ref, acc_ref, carry_ref, *, tq, q_per_kv):
    h = pl.program_id(0)
    qi = pl.program_id(1)
    u = _suffix_ones(tq)
    acc_ref[...] = jnp.zeros_like(acc_ref)
    carry_ref[...] = jnp.zeros_like(carry_ref)

    def block(kj, mask):
        kt = kt_ref[kj]
        r0 = pl.multiple_of(kj * tq, tq)
        v = v_ref[pl.ds(r0, tq), :]
        for g in range(q_per_kv):
            s = _dot(q_ref[:, g * HEAD_DIM:(g + 1) * HEAD_DIM], kt)
            o, carry = _sb_block(s, bias_ref[h * q_per_kv + g], v, u, carry_ref[g], mask)
            acc_ref[g] += o
            carry_ref[g] = carry

    rows = lax.broadcasted_iota(jnp.int32, (tq, tq), 0)
    cols = lax.broadcasted_iota(jnp.int32, (tq, tq), 1)
    block(qi, cols < rows)

    def body(n, _):
        block(qi - 1 - n, None)
        return 0

    lax.fori_loop(0, qi, body, 0)
    o_ref[...] = jnp.concatenate([acc_ref[g] for g in range(q_per_kv)], axis=1).astype(o_ref.dtype)


def _attn_prompt(bias, q, kt, vb, *, t, tq):
    dq = q.shape[1]
    n_kv = kt.shape[1] // HEAD_DIM
    q_per_kv = dq // HEAD_DIM // n_kv
    nb = t // tq
    assert kt.shape[2] == tq
    return pl.pallas_call(
        functools.partial(_attn_prompt_kernel, tq=tq, q_per_kv=q_per_kv),
        grid=(n_kv, nb),
        in_specs=[pl.BlockSpec(memory_space=pltpu.SMEM),
                  pl.BlockSpec((tq, q_per_kv * HEAD_DIM), lambda h, i: (i, h)),
                  pl.BlockSpec((nb, HEAD_DIM, tq), lambda h, i: (0, h, 0)),
                  pl.BlockSpec((t, HEAD_DIM), lambda h, i: (0, h))],
        out_specs=pl.BlockSpec((tq, q_per_kv * HEAD_DIM), lambda h, i: (i, h)),
        out_shape=jax.ShapeDtypeStruct((t, dq), BF16),
        scratch_shapes=[pltpu.VMEM((q_per_kv, tq, HEAD_DIM), F32), pltpu.VMEM((q_per_kv, tq, 1), F32)],
        compiler_params=_cparams("parallel", "arbitrary"), name="attn_prompt",
    )(bias, q, kt, vb)


def _attn_sample_kernel(pt_ref, q_ref, bias_ref, kc_ref, vc_ref, kn_ref, vn_ref, o_ref, acc_ref, carry_ref,
                        *, n_step, page):
    s_id = pl.program_id(1)
    rows = q_ref.shape[1]
    n_kv = kc_ref.shape[2]
    rows_per_kv = rows // n_kv
    row_kv = lax.broadcasted_iota(jnp.int32, (rows, HEAD_DIM), 0) // rows_per_kv
    q = q_ref[0]
    q_bd = jnp.concatenate([jnp.where(row_kv == h, q, jnp.zeros_like(q)) for h in range(n_kv)], axis=1)
    u = _suffix_ones(page)

    def block(k2, v2, mask):
        s = lax.dot_general(q_bd, k2, (((1,), (1,)), ((), ())), preferred_element_type=F32)
        o, carry = _sb_block(s, bias_ref[...], v2, u, carry_ref[...], mask)
        acc_ref[...] += o
        carry_ref[...] = carry

    @pl.when(s_id == 0)
    def _():
        acc_ref[...] = jnp.zeros_like(acc_ref)
        carry_ref[...] = jnp.zeros_like(carry_ref)
        pad = jnp.zeros((page - n_step, n_kv * HEAD_DIM), F32)
        k2 = jnp.concatenate([kn_ref[0], pad], axis=0).astype(BF16)
        v2 = jnp.concatenate([vn_ref[0], pad], axis=0).astype(BF16)
        step = lax.broadcasted_iota(jnp.int32, (rows, page), 0) % n_step
        key = lax.broadcasted_iota(jnp.int32, (rows, page), 1)
        block(k2, v2, key < step)

    @pl.when(s_id > 0)
    def _():
        k2 = jnp.concatenate([kc_ref[0, :, h, :] for h in range(n_kv)], axis=1).astype(BF16)
        v2 = jnp.concatenate([vc_ref[0, :, h, :] for h in range(n_kv)], axis=1).astype(BF16)
        block(k2, v2, None)

    @pl.when(s_id == pl.num_programs(1) - 1)
    def _():
        acc = acc_ref[...]
        o = jnp.zeros((rows, HEAD_DIM), F32)
        for h in range(n_kv):
            o = o + jnp.where(row_kv == h, acc[:, h * HEAD_DIM:(h + 1) * HEAD_DIM], 0.0)
        o_ref[0] = o.astype(o_ref.dtype)


def _attn_sample(page_table, q, bias_col, cache_k, cache_v, k_new, v_new):
    n_seq, rows, _ = q.shape
    n_pages = page_table.shape[1]
    _, page, n_kv, _ = cache_k.shape
    n_step = k_new.shape[1]
    assert page % SUBLANES == 0 and n_step <= page

    def page_map(b, s, pt):
        return (pt[b, n_pages - jnp.maximum(s, 1)], 0, 0, 0)

    per_seq = lambda shape: pl.BlockSpec(shape, lambda b, s, pt: (b,) + (0,) * (len(shape) - 1))
    grid_spec = pltpu.PrefetchScalarGridSpec(
        num_scalar_prefetch=1, grid=(n_seq, n_pages + 1),
        in_specs=[per_seq((1, rows, HEAD_DIM)), pl.BlockSpec((rows, 1), lambda b, s, pt: (0, 0)),
                  pl.BlockSpec((1, page, n_kv, HEAD_DIM), page_map),
                  pl.BlockSpec((1, page, n_kv, HEAD_DIM), page_map),
                  per_seq((1, n_step, n_kv * HEAD_DIM)), per_seq((1, n_step, n_kv * HEAD_DIM))],
        out_specs=per_seq((1, rows, HEAD_DIM)),
        scratch_shapes=[pltpu.VMEM((rows, n_kv * HEAD_DIM), F32), pltpu.VMEM((rows, 1), F32)])
    return pl.pallas_call(
        functools.partial(_attn_sample_kernel, n_step=n_step, page=page),
        grid_spec=grid_spec, out_shape=jax.ShapeDtypeStruct((n_seq, rows, HEAD_DIM), BF16),
        compiler_params=_cparams("parallel", "arbitrary"), name="attn_sample",
    )(page_table, q, bias_col, cache_k, cache_v, k_new, v_new)


def kernel(x_prompt, x_sample, p_prompt, p_sample, state_ssm_re, state_ssm_im, cache_k, cache_v, page_table, s5_lam_re, s5_lam_im, s5_log_dt, s5_b_re, s5_b_im, s5_c_re, s5_c_im, s5_d, s5_w_gate, s5_w_out, w_k, w_v, sb_w_q, sb_w_o, sb_bias, ln_mix_g, ln_mix_b, mlp_w1, mlp_w2, ln_mlp_g, ln_mlp_b, ple_w_proj, ple_w_gate, ln_ple_g, ln_ple_b):
    _, t_p, dm = x_prompt.shape
    n_seq, n_step, _ = x_sample.shape
    n_grp_total, p_state = s5_lam_re.shape[1:]
    n_blk = dm // CH_BLOCK
    gpb = CH_BLOCK // SSM_GROUP
    ns = gpb * p_state
    n_heads = sb_w_q.shape[2] // HEAD_DIM
    rows_s = n_seq * n_step
    assert x_prompt.shape[0] == 1 and n_step == SUBLANES and n_seq % SUBLANES == 0
    bf = lambda a: a.astype(BF16)
    vec = lambda a: a.reshape(1, -1)

    def to_group_order(a):
        c = a.shape[-1]
        return a.reshape(n_seq // SUBLANES, SUBLANES, n_step, c).transpose(0, 2, 1, 3).reshape(rows_s, c)

    def to_batch_order(a):
        c = a.shape[-1]
        return a.reshape(n_seq // SUBLANES, n_step, SUBLANES, c).transpose(0, 2, 1, 3).reshape(n_seq, n_step, c)

    xp = x_prompt.reshape(t_p, dm)
    xs = to_group_order(x_sample)
    x = jnp.concatenate([xp, xs], axis=0)
    p_all = [jnp.concatenate([p_prompt[i].reshape(t_p, -1), to_group_order(p_sample[i])], axis=0)
             for i in range(DEPTH)]

    ab_re, ab_im, bb_re, bb_im = _s5_prep(s5_lam_re[0], s5_lam_im[0], s5_log_dt[0], s5_b_re[0], s5_b_im[0])
    wb = bf(jnp.concatenate([_block_diag_in(bb_re, n_blk, gpb, p_state),
                             _block_diag_in(bb_im, n_blk, gpb, p_state)], axis=-1))
    wc_re = bf(_block_diag_out(s5_c_re[0], n_blk, gpb, p_state))
    wc_im = bf(_block_diag_out(s5_c_im[0], n_blk, gpb, p_state))
    a_re = ab_re.reshape(n_blk, ns)
    a_im = ab_im.reshape(n_blk, ns)
    d = vec(s5_d[0])
    g_p, hp_re, hp_im = _scan_prompt(xp, wb, wc_re, wc_im, a_re, a_im, d, tb=128)
    g_s, hs_re, hs_im = _scan_sample(xs, wb, wc_re, wc_im, a_re, a_im, d,
                                     state_ssm_re[0].reshape(n_seq, -1), state_ssm_im[0].reshape(n_seq, -1),
                                     n_step=n_step)
    g = jnp.concatenate([g_p, g_s], axis=0)
    x = _glu(g, x, bf(s5_w_gate[0]), bf(s5_w_out[0]), vec(ln_mix_g[0]), vec(ln_mix_b[0]), tm=512, tf=1024)
    x = _mlp(x, bf(mlp_w1[0]), bf(mlp_w2[0]), vec(ln_mlp_g[0]), vec(ln_mlp_b[0]), tm=512, tf=1024)
    x = _ple(x, p_all[0], bf(ple_w_proj[0]), bf(ple_w_gate[0]), vec(ln_ple_g[0]), vec(ln_ple_b[0]), tm=256)

    tq = 256
    q, k, v, kt, vb = _qkv(x, bf(sb_w_q[0]), bf(w_k), bf(w_v), bf(w_k.T), tm=tq)
    k_s = to_batch_order(k[t_p:])
    v_s = to_batch_order(v[t_p:])
    o_p = _attn_prompt(sb_bias[0], q, kt, vb, t=t_p, tq=tq)
    q_s = to_batch_order(q[t_p:]).reshape(n_seq, n_step, n_heads, HEAD_DIM).transpose(0, 2, 1, 3)
    bias_col = jnp.repeat(sb_bias[0], n_step).reshape(n_heads * n_step, 1)
    o_s = _attn_sample(page_table, q_s.reshape(n_seq, n_heads * n_step, HEAD_DIM), bias_col,
                       cache_k, cache_v, k_s, v_s)
    o_s = o_s.reshape(n_seq, n_heads, n_step, HEAD_DIM).transpose(0, 2, 1, 3).reshape(n_seq, n_step, -1)
    o = jnp.concatenate([o_p, to_group_order(o_s)], axis=0)
    x = _attn_out(o, x, bf(sb_w_o[0]), vec(ln_mix_g[1]), vec(ln_mix_b[1]), tm=512)
    x = _mlp(x, bf(mlp_w1[1]), bf(mlp_w2[1]), vec(ln_mlp_g[1]), vec(ln_mlp_b[1]), tm=512, tf=1024)
    x = _ple(x, p_all[1], bf(ple_w_proj[1]), bf(ple_w_gate[1]), vec(ln_ple_g[1]), vec(ln_ple_b[1]), tm=256)

    state = lambda a: a.reshape(1, -1, n_grp_total, p_state)
    kv = lambda a, lead: a.reshape(lead + (N_KV_HEADS, HEAD_DIM))
    return (x[:t_p].reshape(1, t_p, dm), to_batch_order(x[t_p:]),
            state(hp_re), state(hp_im), kv(k[:t_p], (1, t_p)), kv(v[:t_p], (1, t_p)),
            state(hs_re), state(hs_im), kv(k_s, (n_seq, n_step)), kv(v_s, (n_seq, n_step)))
```

```python
import functools
import math

import jax
import jax.numpy as jnp
from jax import lax
from jax.experimental import pallas as pl
from jax.experimental.pallas import tpu as pltpu

F32 = jnp.float32
BF16 = jnp.bfloat16

SSM_GROUP = 16
HEAD_DIM = 128
N_KV_HEADS = 4
LN_EPS = 1e-5
DEPTH = 2
DN_ALPHA = (2.0 * DEPTH) ** 0.25

SUBLANES = 8
LANES = 128
CH_BLOCK = 256
STEP_PITCH = 17
BLOCK_PITCH = 2
VMEM_LIMIT = 56 * 1024 * 1024


def _cparams(*sem):
    return pltpu.CompilerParams(dimension_semantics=sem, vmem_limit_bytes=VMEM_LIMIT)


def _dot(a, b):
    return jnp.dot(a, b, preferred_element_type=F32)


def _sigmoid(x):
    return 1.0 / (1.0 + jnp.exp(-x))


def _layer_norm_rows(y, g, b):
    mu = jnp.mean(y, axis=-1, keepdims=True)
    yc = y - mu
    var = jnp.mean(yc * yc, axis=-1, keepdims=True)
    return yc * lax.rsqrt(var + LN_EPS) * g + b


def _discretize(lam_re, lam_im, dt):
    mag = jnp.exp(lam_re * dt)
    ang = lam_im * dt
    ab_re = mag * jnp.cos(ang)
    ab_im = mag * jnp.sin(ang)
    den = lam_re * lam_re + lam_im * lam_im
    nr = ab_re - 1.0
    f_re = (nr * lam_re + ab_im * lam_im) / den
    f_im = (ab_im * lam_re - nr * lam_im) / den
    return ab_re, ab_im, f_re, f_im


def _s5_prep_kernel(lre_ref, lim_ref, ldt_ref, lre_rep_ref, lim_rep_ref, bre_ref, bim_ref,
                    abre_ref, abim_ref, bbre_ref, bbim_ref):
    dt = jnp.exp(ldt_ref[...])
    ab_re, ab_im, _, _ = _discretize(lre_ref[...], lim_ref[...], dt)
    abre_ref[...] = ab_re
    abim_ref[...] = ab_im
    _, _, f_re, f_im = _discretize(lre_rep_ref[...], lim_rep_ref[...], dt)
    b_re = bre_ref[...]
    b_im = bim_ref[...]
    bbre_ref[...] = f_re * b_re - f_im * b_im
    bbim_ref[...] = f_re * b_im + f_im * b_re


def _s5_prep(lam_re, lam_im, log_dt, b_re, b_im):
    g, p = lam_re.shape
    rep = lambda a: jnp.repeat(a, SSM_GROUP, axis=1)
    out = [jax.ShapeDtypeStruct((g, p), F32)] * 2 + [jax.ShapeDtypeStruct((g, p * SSM_GROUP), F32)] * 2
    return pl.pallas_call(_s5_prep_kernel, out_shape=out, name="s5_prep")(
        lam_re, lam_im, log_dt.reshape(g, 1), rep(lam_re), rep(lam_im),
        b_re.reshape(g, p * SSM_GROUP), b_im.reshape(g, p * SSM_GROUP))


def _block_diag_in(bb, n_blk, gpb, p):
    v = bb.reshape(n_blk, gpb, p, SSM_GROUP).transpose(0, 1, 3, 2)
    eye = jnp.eye(gpb, dtype=bb.dtype)
    w = v[:, :, :, None, :] * eye[None, :, None, :, None]
    return w.reshape(n_blk, gpb * SSM_GROUP, gpb * p)


def _block_diag_out(c, n_blk, gpb, p):
    v = c.reshape(n_blk, gpb, SSM_GROUP, p).transpose(0, 1, 3, 2)
    eye = jnp.eye(gpb, dtype=c.dtype)
    w = v[:, :, :, None, :] * eye[None, :, None, :, None]
    return w.reshape(n_blk, gpb * p, gpb * SSM_GROUP)


def _complex_step(a_re, a_im, h_re, h_im, b_re, b_im):
    return (a_re * h_re - a_im * h_im + b_re, a_re * h_im + a_im * h_re + b_im)


def _scan_prompt_kernel(x_ref, wb_ref, wcre_ref, wcim_ref, are_ref, aim_ref, d_ref,
                        g_ref, hre_ref, him_ref, s_ref, h_ref, *, n_blk, tb, ns):
    nc = ns // LANES
    block_rows = lambda j: pl.ds(j * BLOCK_PITCH, tb, stride=STEP_PITCH)

    @pl.when(pl.program_id(0) == 0)
    def _():
        h_ref[...] = jnp.zeros_like(h_ref)

    x = x_ref[...]
    for j in range(n_blk):
        u = x[:, j * CH_BLOCK:(j + 1) * CH_BLOCK].astype(BF16)
        bu = _dot(u, wb_ref[j])
        for c in range(2 * nc):
            s_ref[c, block_rows(j), :] = bu[:, c * LANES:(c + 1) * LANES]
    a_re = are_ref[...]
    a_im = aim_ref[...]

    def body(t, carry):
        rows = pl.ds(t * STEP_PITCH, n_blk, stride=BLOCK_PITCH)
        b_re = jnp.concatenate([s_ref[c, rows, :] for c in range(nc)], axis=1)
        b_im = jnp.concatenate([s_ref[nc + c, rows, :] for c in range(nc)], axis=1)
        h_re, h_im = _complex_step(a_re, a_im, carry[0], carry[1], b_re, b_im)
        for c in range(nc):
            s_ref[c, rows, :] = h_re[:, c * LANES:(c + 1) * LANES]
            s_ref[nc + c, rows, :] = h_im[:, c * LANES:(c + 1) * LANES]
        return h_re, h_im

    h_re, h_im = lax.fori_loop(0, tb, body, (h_ref[:, :ns], h_ref[:, ns:]), unroll=8)
    h_ref[...] = jnp.concatenate([h_re, h_im], axis=1)
    hre_ref[...] = h_re
    him_ref[...] = h_im

    ys = []
    for j in range(n_blk):
        hj_re = jnp.concatenate([s_ref[c, block_rows(j), :] for c in range(nc)], axis=1)
        hj_im = jnp.concatenate([s_ref[nc + c, block_rows(j), :] for c in range(nc)], axis=1)
        ys.append(_dot(hj_re.astype(BF16), wcre_ref[j]) - _dot(hj_im.astype(BF16), wcim_ref[j]))
    y = jnp.concatenate(ys, axis=1) + d_ref[...] * x
    g_ref[...] = jax.nn.gelu(y).astype(g_ref.dtype)


def _scan_prompt(x, wb, wc_re, wc_im, a_re, a_im, d, *, tb):
    t, dm = x.shape
    n_blk, _, ns2 = wb.shape
    ns = ns2 // 2
    assert n_blk == SUBLANES and t % tb == 0
    assert (n_blk - 1) * BLOCK_PITCH < STEP_PITCH
    full = lambda shape: pl.BlockSpec(shape, lambda i: (0,) * len(shape))
    const = lambda shape: pl.BlockSpec(shape, lambda i: (0,) * len(shape), pipeline_mode=pl.Buffered(1))
    return pl.pallas_call(
        functools.partial(_scan_prompt_kernel, n_blk=n_blk, tb=tb, ns=ns),
        grid=(t // tb,),
        in_specs=[pl.BlockSpec((tb, dm), lambda i: (i, 0)), const(wb.shape), const(wc_re.shape),
                  const(wc_im.shape), full(a_re.shape), full(a_im.shape), full(d.shape)],
        out_specs=[pl.BlockSpec((tb, dm), lambda i: (i, 0)), full((n_blk, ns)), full((n_blk, ns))],
        out_shape=[jax.ShapeDtypeStruct((t, dm), BF16), jax.ShapeDtypeStruct((n_blk, ns), F32),
                   jax.ShapeDtypeStruct((n_blk, ns), F32)],
        scratch_shapes=[pltpu.VMEM((ns2 // LANES, tb * STEP_PITCH, LANES), F32), pltpu.VMEM((n_blk, ns2), F32)],
        compiler_params=_cparams("arbitrary"),
        name="s5_scan_prompt",
    )(x, wb, wc_re, wc_im, a_re, a_im, d)


def _scan_sample_kernel(x_ref, wb_ref, wcre_ref, wcim_ref, are_ref, aim_ref, d_ref, h0re_ref, h0im_ref,
                        g_ref, hre_ref, him_ref, s_ref, *, n_grp, n_step, ns):
    x = x_ref[...]
    s_ref[...] = _dot(x.astype(BF16), wb_ref[0])
    a_re = jnp.broadcast_to(are_ref[0], (SUBLANES, ns))
    a_im = jnp.broadcast_to(aim_ref[0], (SUBLANES, ns))

    def body(bg, _):
        b0 = pl.multiple_of(bg * SUBLANES, SUBLANES)
        h_re = h0re_ref[pl.ds(b0, SUBLANES), :]
        h_im = h0im_ref[pl.ds(b0, SUBLANES), :]
        for t in range(n_step):
            r0 = pl.multiple_of((bg * n_step + t) * SUBLANES, SUBLANES)
            tile = s_ref[pl.ds(r0, SUBLANES), :]
            h_re, h_im = _complex_step(a_re, a_im, h_re, h_im, tile[:, :ns], tile[:, ns:])
            s_ref[pl.ds(r0, SUBLANES), :] = jnp.concatenate([h_re, h_im], axis=1)
        hre_ref[pl.ds(b0, SUBLANES), :] = h_re
        him_ref[pl.ds(b0, SUBLANES), :] = h_im
        return 0

    lax.fori_loop(0, n_grp, body, 0)
    h = s_ref[...]
    y = _dot(h[:, :ns].astype(BF16), wcre_ref[0]) - _dot(h[:, ns:].astype(BF16), wcim_ref[0])
    g_ref[...] = jax.nn.gelu(y + d_ref[...] * x).astype(g_ref.dtype)


def _scan_sample(x, wb, wc_re, wc_im, a_re, a_im, d, h0_re, h0_im, *, n_step):
    rows, dm = x.shape
    n_blk, _, ns2 = wb.shape
    ns = ns2 // 2
    n_seq = h0_re.shape[0]
    assert rows == n_seq * n_step and n_seq % SUBLANES == 0
    col = lambda shape: pl.BlockSpec(shape, lambda j: (0, j))
    lead = lambda shape: pl.BlockSpec(shape, lambda j: (j,) + (0,) * (len(shape) - 1))
    return pl.pallas_call(
        functools.partial(_scan_sample_kernel, n_grp=n_seq // SUBLANES, n_step=n_step, ns=ns),
        grid=(n_blk,),
        in_specs=[col((rows, CH_BLOCK)), lead((1, CH_BLOCK, ns2)), lead((1, ns, CH_BLOCK)),
                  lead((1, ns, CH_BLOCK)), lead((1, 1, ns)), lead((1, 1, ns)), col((1, CH_BLOCK)),
                  col((n_seq, ns)), col((n_seq, ns))],
        out_specs=[col((rows, CH_BLOCK)), col((n_seq, ns)), col((n_seq, ns))],
        out_shape=[jax.ShapeDtypeStruct((rows, dm), BF16), jax.ShapeDtypeStruct((n_seq, n_blk * ns), F32),
                   jax.ShapeDtypeStruct((n_seq, n_blk * ns), F32)],
        scratch_shapes=[pltpu.VMEM((rows, ns2), F32)],
        compiler_params=_cparams("arbitrary"),
        name="s5_scan_sample",
    )(x, wb, wc_re, wc_im, a_re.reshape(n_blk, 1, ns), a_im.reshape(n_blk, 1, ns), d, h0_re, h0_im)


def _post_norm_store(o_ref, x, sub, g_ref, b_ref):
    o_ref[...] = _layer_norm_rows(DN_ALPHA * x + sub, g_ref[...], b_ref[...])


def _glu_kernel(a_ref, ablk_ref, x_ref, w1_ref, w2_ref, g_ref, b_ref, o_ref, acc_ref):
    f = pl.program_id(1)

    @pl.when(f == 0)
    def _():
        acc_ref[...] = jnp.zeros_like(acc_ref)

    h = ablk_ref[...].astype(F32) * _sigmoid(_dot(a_ref[...], w1_ref[...]))
    acc_ref[...] += _dot(h.astype(BF16), w2_ref[...])

    @pl.when(f == pl.num_programs(1) - 1)
    def _():
        _post_norm_store(o_ref, x_ref[...], acc_ref[...], g_ref, b_ref)


def _mlp_kernel(x_ref, w1_ref, w2_ref, g_ref, b_ref, o_ref, acc_ref):
    f = pl.program_id(1)

    @pl.when(f == 0)
    def _():
        acc_ref[...] = jnp.zeros_like(acc_ref)

    h = jnp.square(jnp.maximum(_dot(x_ref[...].astype(BF16), w1_ref[...]), 0.0))
    acc_ref[...] += _dot(h.astype(BF16), w2_ref[...])

    @pl.when(f == pl.num_programs(1) - 1)
    def _():
        _post_norm_store(o_ref, x_ref[...], acc_ref[...], g_ref, b_ref)


def _two_matmul_specs(tm, dm, tf):
    row = pl.BlockSpec((tm, dm), lambda i, f: (i, 0))
    w1 = pl.BlockSpec((dm, tf), lambda i, f: (0, f))
    w2 = pl.BlockSpec((tf, dm), lambda i, f: (f, 0))
    vec = pl.BlockSpec((1, dm), lambda i, f: (0, 0))
    return row, w1, w2, vec


def _glu(a, x, w1, w2, g, b, *, tm, tf):
    t, dm = x.shape
    row, w1s, w2s, vec = _two_matmul_specs(tm, dm, tf)
    return pl.pallas_call(
        _glu_kernel, grid=(t // tm, w1.shape[1] // tf),
        in_specs=[row, pl.BlockSpec((tm, tf), lambda i, f: (i, f)), row, w1s, w2s, vec, vec],
        out_specs=row, out_shape=jax.ShapeDtypeStruct((t, dm), F32),
        scratch_shapes=[pltpu.VMEM((tm, dm), F32)],
        compiler_params=_cparams("parallel", "arbitrary"), name="glu",
    )(a, a, x, w1, w2, g, b)


def _mlp(x, w1, w2, g, b, *, tm, tf):
    t, dm = x.shape
    row, w1s, w2s, vec = _two_matmul_specs(tm, dm, tf)
    return pl.pallas_call(
        _mlp_kernel, grid=(t // tm, w1.shape[1] // tf),
        in_specs=[row, w1s, w2s, vec, vec],
        out_specs=row, out_shape=jax.ShapeDtypeStruct((t, dm), F32),
        scratch_shapes=[pltpu.VMEM((tm, dm), F32)],
        compiler_params=_cparams("parallel", "arbitrary"), name="mlp",
    )(x, w1, w2, g, b)


def _ple_kernel(x_ref, p_ref, wp_ref, wg_ref, g_ref, b_ref, o_ref):
    x = x_ref[...]
    sub = _dot(p_ref[...].astype(BF16), wp_ref[...]) * _sigmoid(_dot(x.astype(BF16), wg_ref[...]))
    _post_norm_store(o_ref, x, sub, g_ref, b_ref)


def _ple(x, p, wp, wg, g, b, *, tm):
    t, dm = x.shape
    row = pl.BlockSpec((tm, dm), lambda i: (i, 0))
    full = lambda a: pl.BlockSpec(a.shape, lambda i: (0, 0))
    return pl.pallas_call(
        _ple_kernel, grid=(t // tm,),
        in_specs=[row, pl.BlockSpec((tm, p.shape[1]), lambda i: (i, 0)), full(wp), full(wg), full(g), full(b)],
        out_specs=row, out_shape=jax.ShapeDtypeStruct((t, dm), F32),
        compiler_params=_cparams("parallel"), name="ple",
    )(x, p, wp, wg, g, b)


def _attn_out_kernel(o_ref, x_ref, w_ref, g_ref, b_ref, y_ref):
    _post_norm_store(y_ref, x_ref[...], _dot(o_ref[...], w_ref[...]), g_ref, b_ref)


def _attn_out(o, x, w, g, b, *, tm):
    t, dm = x.shape
    row = pl.BlockSpec((tm, dm), lambda i: (i, 0))
    full = lambda a: pl.BlockSpec(a.shape, lambda i: (0, 0))
    return pl.pallas_call(
        _attn_out_kernel, grid=(t // tm,),
        in_specs=[row, row, full(w), full(g), full(b)],
        out_specs=row, out_shape=jax.ShapeDtypeStruct((t, dm), F32),
        compiler_params=_cparams("parallel"), name="attn_out",
    )(o, x, w, g, b)


def _qkv_kernel(x_ref, wq_ref, wk_ref, wv_ref, wkt_ref, q_ref, k_ref, v_ref, kt_ref, vb_ref):
    xb = x_ref[...].astype(BF16)
    q_ref[...] = (_dot(xb, wq_ref[...]) * (HEAD_DIM ** -0.5)).astype(q_ref.dtype)
    k_ref[...] = _dot(xb, wk_ref[...])
    v = _dot(xb, wv_ref[...])
    v_ref[...] = v
    vb_ref[...] = v.astype(vb_ref.dtype)
    kt = lax.dot_general(wkt_ref[...], xb, (((1,), (1,)), ((), ())), preferred_element_type=F32)
    kt_ref[0] = kt.astype(kt_ref.dtype)


def _qkv(x, wq, wk, wv, wkt, *, tm):
    t, dm = x.shape
    dq, dkv = wq.shape[1], wk.shape[1]
    row = lambda n: pl.BlockSpec((tm, n), lambda i: (i, 0))
    full = lambda a: pl.BlockSpec(a.shape, lambda i: (0, 0))
    return pl.pallas_call(
        _qkv_kernel, grid=(t // tm,),
        in_specs=[row(dm), full(wq), full(wk), full(wv), full(wkt)],
        out_specs=[row(dq), row(dkv), row(dkv), pl.BlockSpec((1, dkv, tm), lambda i: (i, 0, 0)), row(dkv)],
        out_shape=[jax.ShapeDtypeStruct((t, dq), BF16), jax.ShapeDtypeStruct((t, dkv), F32),
                   jax.ShapeDtypeStruct((t, dkv), F32), jax.ShapeDtypeStruct((t // tm, dkv, tm), BF16),
                   jax.ShapeDtypeStruct((t, dkv), BF16)],
        compiler_params=_cparams("parallel"), name="qkv",
    )(x, wq, wk, wv, wkt)


MASKED_LOGIT = -1e30


def _softplus(z):
    return jnp.maximum(z, 0.0) + jnp.log(1.0 + jnp.exp(-jnp.abs(z)))


def _suffix_ones(n):
    return (lax.broadcasted_iota(jnp.int32, (n, n), 0) >= lax.broadcasted_iota(jnp.int32, (n, n), 1)).astype(BF16)


def _sb_scores(s, bias, mask):
    z = s + bias
    if mask is not None:
        z = jnp.where(mask, z, MASKED_LOGIT)
    sp = _softplus(z)
    hi = sp.astype(BF16)
    return z, hi, (sp - hi.astype(F32)).astype(BF16)


def _sb_weights(z, hi, lo, u, carry):
    c = _dot(hi, u) + _dot(lo, u) + carry
    return jnp.exp(z - c).astype(BF16), c[:, :1]


def _attn_prompt_kernel(bias_ref, q_ref, kt_ref, v_ref, o_ref, acc_ref, carry_ref,
                        z0_ref, hi0_ref, lo0_ref, z1_ref, hi1_ref, lo1_ref, *, tq, q_per_kv):
    h = pl.program_id(0)
    qi = pl.program_id(1)
    u = _suffix_ones(tq)
    acc_ref[...] = jnp.zeros_like(acc_ref)
    carry_ref[...] = jnp.zeros_like(carry_ref)
    stage = ((z0_ref, hi0_ref, lo0_ref), (z1_ref, hi1_ref, lo1_ref))

    def score(kj, slot, mask):
        z_ref, hi_ref, lo_ref = stage[slot]
        kt = kt_ref[kj]
        for g in range(q_per_kv):
            s = _dot(q_ref[:, g * HEAD_DIM:(g + 1) * HEAD_DIM], kt)
            z_ref[g], hi_ref[g], lo_ref[g] = _sb_scores(s, bias_ref[h * q_per_kv + g], mask)

    def accumulate(kj, slot):
        z_ref, hi_ref, lo_ref = stage[slot]
        v = v_ref[pl.ds(pl.multiple_of(kj * tq, tq), tq), :]
        for g in range(q_per_kv):
            w, carry = _sb_weights(z_ref[g], hi_ref[g], lo_ref[g], u, carry_ref[g])
            acc_ref[g] += _dot(w, v)
            carry_ref[g] = carry

    rows = lax.broadcasted_iota(jnp.int32, (tq, tq), 0)
    cols = lax.broadcasted_iota(jnp.int32, (tq, tq), 1)
    score(qi, 0, cols < rows)

    def pair(m, _):
        kj = qi - 2 * m
        accumulate(kj, 0)
        score(kj - 1, 1, None)
        accumulate(kj - 1, 1)
        score(kj - 2, 0, None)
        return 0

    lax.fori_loop(0, qi // 2, pair, 0)
    odd = qi % 2

    @pl.when(odd == 1)
    def _():
        accumulate(1, 0)
        score(0, 1, None)
        accumulate(0, 1)

    @pl.when(odd == 0)
    def _():
        accumulate(0, 0)

    o_ref[...] = jnp.concatenate([acc_ref[g] for g in range(q_per_kv)], axis=1).astype(o_ref.dtype)


def _attn_prompt(bias, q, kt, vb, *, t, tq):
    dq = q.shape[1]
    n_kv = kt.shape[1] // HEAD_DIM
    q_per_kv = dq // HEAD_DIM // n_kv
    nb = t // tq
    assert kt.shape[2] == tq
    return pl.pallas_call(
        functools.partial(_attn_prompt_kernel, tq=tq, q_per_kv=q_per_kv),
        grid=(n_kv, nb),
        in_specs=[pl.BlockSpec(memory_space=pltpu.SMEM),
                  pl.BlockSpec((tq, q_per_kv * HEAD_DIM), lambda h, i: (i, h)),
                  pl.BlockSpec((nb, HEAD_DIM, tq), lambda h, i: (0, h, 0)),
                  pl.BlockSpec((t, HEAD_DIM), lambda h, i: (0, h))],
        out_specs=pl.BlockSpec((tq, q_per_kv * HEAD_DIM), lambda h, i: (i, h)),
        out_shape=jax.ShapeDtypeStruct((t, dq), BF16),
        scratch_shapes=[pltpu.VMEM((q_per_kv, tq, HEAD_DIM), F32), pltpu.VMEM((q_per_kv, tq, 1), F32)]
        + [pltpu.VMEM((q_per_kv, tq, tq), dt) for dt in (F32, BF16, BF16)] * 2,
        compiler_params=_cparams("parallel", "arbitrary"), name="attn_prompt",
    )(bias, q, kt, vb)


def _attn_sample_kernel(pt_ref, q_ref, bias_ref, kc_ref, vc_ref, kn_ref, vn_ref, o_ref, acc_ref, carry_ref,
                        *, n_step, page):
    s_id = pl.program_id(1)
    rows = q_ref.shape[1]
    n_kv = kc_ref.shape[2]
    rows_per_kv = rows // n_kv
    row_kv = lax.broadcasted_iota(jnp.int32, (rows, HEAD_DIM), 0) // rows_per_kv
    q = q_ref[0]
    q_bd = jnp.concatenate([jnp.where(row_kv == h, q, jnp.zeros_like(q)) for h in range(n_kv)], axis=1)
    u = _suffix_ones(page)

    def block(k2, v2, mask):
        s = lax.dot_general(q_bd, k2, (((1,), (1,)), ((), ())), preferred_element_type=F32)
        w, carry = _sb_weights(*_sb_scores(s, bias_ref[...], mask), u, carry_ref[...])
        acc_ref[...] += _dot(w, v2)
        carry_ref[...] = carry

    @pl.when(s_id == 0)
    def _():
        acc_ref[...] = jnp.zeros_like(acc_ref)
        carry_ref[...] = jnp.zeros_like(carry_ref)
        pad = jnp.zeros((page - n_step, n_kv * HEAD_DIM), F32)
        k2 = jnp.concatenate([kn_ref[0], pad], axis=0).astype(BF16)
        v2 = jnp.concatenate([vn_ref[0], pad], axis=0).astype(BF16)
        step = lax.broadcasted_iota(jnp.int32, (rows, page), 0) % n_step
        key = lax.broadcasted_iota(jnp.int32, (rows, page), 1)
        block(k2, v2, key < step)

    @pl.when(s_id > 0)
    def _():
        k2 = jnp.concatenate([kc_ref[0, :, h, :] for h in range(n_kv)], axis=1).astype(BF16)
        v2 = jnp.concatenate([vc_ref[0, :, h, :] for h in range(n_kv)], axis=1).astype(BF16)
        block(k2, v2, None)

    @pl.when(s_id == pl.num_programs(1) - 1)
    def _():
        acc = acc_ref[...]
        o = jnp.zeros((rows, HEAD_DIM), F32)
        for h in range(n_kv):
            o = o + jnp.where(row_kv == h, acc[:, h * HEAD_DIM:(h + 1) * HEAD_DIM], 0.0)
        o_ref[0] = o.astype(o_ref.dtype)


def _attn_sample(page_table, q, bias_col, cache_k, cache_v, k_new, v_new):
    n_seq, rows, _ = q.shape
    n_pages = page_table.shape[1]
    _, page, n_kv, _ = cache_k.shape
    n_step = k_new.shape[1]
    assert page % SUBLANES == 0 and n_step <= page

    def page_map(b, s, pt):
        return (pt[b, n_pages - jnp.maximum(s, 1)], 0, 0, 0)

    per_seq = lambda shape: pl.BlockSpec(shape, lambda b, s, pt: (b,) + (0,) * (len(shape) - 1))
    grid_spec = pltpu.PrefetchScalarGridSpec(
        num_scalar_prefetch=1, grid=(n_seq, n_pages + 1),
        in_specs=[per_seq((1, rows, HEAD_DIM)), pl.BlockSpec((rows, 1), lambda b, s, pt: (0, 0)),
                  pl.BlockSpec((1, page, n_kv, HEAD_DIM), page_map),
                  pl.BlockSpec((1, page, n_kv, HEAD_DIM), page_map),
                  per_seq((1, n_step, n_kv * HEAD_DIM)), per_seq((1, n_step, n_kv * HEAD_DIM))],
        out_specs=per_seq((1, rows, HEAD_DIM)),
        scratch_shapes=[pltpu.VMEM((rows, n_kv * HEAD_DIM), F32), pltpu.VMEM((rows, 1), F32)])
    return pl.pallas_call(
        functools.partial(_attn_sample_kernel, n_step=n_step, page=page),
        grid_spec=grid_spec, out_shape=jax.ShapeDtypeStruct((n_seq, rows, HEAD_DIM), BF16),
        compiler_params=_cparams("parallel", "arbitrary"), name="attn_sample",
    )(page_table, q, bias_col, cache_k, cache_v, k_new, v_new)


def kernel(x_prompt, x_sample, p_prompt, p_sample, state_ssm_re, state_ssm_im, cache_k, cache_v, page_table, s5_lam_re, s5_lam_im, s5_log_dt, s5_b_re, s5_b_im, s5_c_re, s5_c_im, s5_d, s5_w_gate, s5_w_out, w_k, w_v, sb_w_q, sb_w_o, sb_bias, ln_mix_g, ln_mix_b, mlp_w1, mlp_w2, ln_mlp_g, ln_mlp_b, ple_w_proj, ple_w_gate, ln_ple_g, ln_ple_b):
    _, t_p, dm = x_prompt.shape
    n_seq, n_step, _ = x_sample.shape
    n_grp_total, p_state = s5_lam_re.shape[1:]
    n_blk = dm // CH_BLOCK
    gpb = CH_BLOCK // SSM_GROUP
    ns = gpb * p_state
    n_heads = sb_w_q.shape[2] // HEAD_DIM
    rows_s = n_seq * n_step
    assert x_prompt.shape[0] == 1 and n_step == SUBLANES and n_seq % SUBLANES == 0
    bf = lambda a: a.astype(BF16)
    vec = lambda a: a.reshape(1, -1)

    def to_group_order(a):
        c = a.shape[-1]
        return a.reshape(n_seq // SUBLANES, SUBLANES, n_step, c).transpose(0, 2, 1, 3).reshape(rows_s, c)

    def to_batch_order(a):
        c = a.shape[-1]
        return a.reshape(n_seq // SUBLANES, n_step, SUBLANES, c).transpose(0, 2, 1, 3).reshape(n_seq, n_step, c)

    xp = x_prompt.reshape(t_p, dm)
    xs = to_group_order(x_sample)
    x = jnp.concatenate([xp, xs], axis=0)
    p_all = [jnp.concatenate([p_prompt[i].reshape(t_p, -1), to_group_order(p_sample[i])], axis=0)
             for i in range(DEPTH)]

    ab_re, ab_im, bb_re, bb_im = _s5_prep(s5_lam_re[0], s5_lam_im[0], s5_log_dt[0], s5_b_re[0], s5_b_im[0])
    wb = bf(jnp.concatenate([_block_diag_in(bb_re, n_blk, gpb, p_state),
                             _block_diag_in(bb_im, n_blk, gpb, p_state)], axis=-1))
    wc_re = bf(_block_diag_out(s5_c_re[0], n_blk, gpb, p_state))
    wc_im = bf(_block_diag_out(s5_c_im[0], n_blk, gpb, p_state))
    a_re = ab_re.reshape(n_blk, ns)
    a_im = ab_im.reshape(n_blk, ns)
    d = vec(s5_d[0])
    g_p, hp_re, hp_im = _scan_prompt(xp, wb, wc_re, wc_im, a_re, a_im, d, tb=128)
    g_s, hs_re, hs_im = _scan_sample(xs, wb, wc_re, wc_im, a_re, a_im, d,
                                     state_ssm_re[0].reshape(n_seq, -1), state_ssm_im[0].reshape(n_seq, -1),
                                     n_step=n_step)
    g = jnp.concatenate([g_p, g_s], axis=0)
    x = _glu(g, x, bf(s5_w_gate[0]), bf(s5_w_out[0]), vec(ln_mix_g[0]), vec(ln_mix_b[0]), tm=512, tf=1024)
    x = _mlp(x, bf(mlp_w1[0]), bf(mlp_w2[0]), vec(ln_mlp_g[0]), vec(ln_mlp_b[0]), tm=512, tf=1024)
    x = _ple(x, p_all[0], bf(ple_w_proj[0]), bf(ple_w_gate[0]), vec(ln_ple_g[0]), vec(ln_ple_b[0]), tm=256)

    tq = 256
    q, k, v, kt, vb = _qkv(x, bf(sb_w_q[0]), bf(w_k), bf(w_v), bf(w_k.T), tm=tq)
    k_s = to_batch_order(k[t_p:])
    v_s = to_batch_order(v[t_p:])
    o_p = _attn_prompt(sb_bias[0], q, kt, vb, t=t_p, tq=tq)
    q_s = to_batch_order(q[t_p:]).reshape(n_seq, n_step, n_heads, HEAD_DIM).transpose(0, 2, 1, 3)
    bias_col = jnp.repeat(sb_bias[0], n_step).reshape(n_heads * n_step, 1)
    o_s = _attn_sample(page_table, q_s.reshape(n_seq, n_heads * n_step, HEAD_DIM), bias_col,
                       cache_k, cache_v, k_s, v_s)
    o_s = o_s.reshape(n_seq, n_heads, n_step, HEAD_DIM).transpose(0, 2, 1, 3).reshape(n_seq, n_step, -1)
    o = jnp.concatenate([o_p, to_group_order(o_s)], axis=0)
    x = _attn_out(o, x, bf(sb_w_o[0]), vec(ln_mix_g[1]), vec(ln_mix_b[1]), tm=512)
    x = _mlp(x, bf(mlp_w1[1]), bf(mlp_w2[1]), vec(ln_mlp_g[1]), vec(ln_mlp_b[1]), tm=512, tf=1024)
    x = _ple(x, p_all[1], bf(ple_w_proj[1]), bf(ple_w_gate[1]), vec(ln_ple_g[1]), vec(ln_ple_b[1]), tm=256)

    state = lambda a: a.reshape(1, -1, n_grp_total, p_state)
    kv = lambda a, lead: a.reshape(lead + (N_KV_HEADS, HEAD_DIM))
    return (x[:t_p].reshape(1, t_p, dm), to_batch_order(x[t_p:]),
            state(hp_re), state(hp_im), kv(k[:t_p], (1, t_p)), kv(v[:t_p], (1, t_p)),
            state(hs_re), state(hs_im), kv(k_s, (n_seq, n_step)), kv(v_s, (n_seq, n_step)))
```

```python
import functools
import math

import jax
import jax.numpy as jnp
from jax import lax
from jax.experimental import pallas as pl
from jax.experimental.pallas import tpu as pltpu

F32 = jnp.float32
BF16 = jnp.bfloat16

SSM_GROUP = 16
HEAD_DIM = 128
N_KV_HEADS = 4
LN_EPS = 1e-5
DEPTH = 2
DN_ALPHA = (2.0 * DEPTH) ** 0.25

SUBLANES = 8
LANES = 128
CH_BLOCK = 256
STEP_PITCH = 17
BLOCK_PITCH = 2
VMEM_LIMIT = 56 * 1024 * 1024


def _cparams(*sem):
    return pltpu.CompilerParams(dimension_semantics=sem, vmem_limit_bytes=VMEM_LIMIT)


def _dot(a, b):
    return jnp.dot(a, b, preferred_element_type=F32)


def _sigmoid(x):
    return 1.0 / (1.0 + jnp.exp(-x))


def _layer_norm_rows(y, g, b):
    mu = jnp.mean(y, axis=-1, keepdims=True)
    yc = y - mu
    var = jnp.mean(yc * yc, axis=-1, keepdims=True)
    return yc * lax.rsqrt(var + LN_EPS) * g + b


def _discretize(lam_re, lam_im, dt):
    mag = jnp.exp(lam_re * dt)
    ang = lam_im * dt
    ab_re = mag * jnp.cos(ang)
    ab_im = mag * jnp.sin(ang)
    den = lam_re * lam_re + lam_im * lam_im
    nr = ab_re - 1.0
    f_re = (nr * lam_re + ab_im * lam_im) / den
    f_im = (ab_im * lam_re - nr * lam_im) / den
    return ab_re, ab_im, f_re, f_im


def _s5_prep_kernel(lre_ref, lim_ref, ldt_ref, lre_rep_ref, lim_rep_ref, bre_ref, bim_ref,
                    abre_ref, abim_ref, bbre_ref, bbim_ref):
    dt = jnp.exp(ldt_ref[...])
    ab_re, ab_im, _, _ = _discretize(lre_ref[...], lim_ref[...], dt)
    abre_ref[...] = ab_re
    abim_ref[...] = ab_im
    _, _, f_re, f_im = _discretize(lre_rep_ref[...], lim_rep_ref[...], dt)
    b_re = bre_ref[...]
    b_im = bim_ref[...]
    bbre_ref[...] = f_re * b_re - f_im * b_im
    bbim_ref[...] = f_re * b_im + f_im * b_re


def _s5_prep(lam_re, lam_im, log_dt, b_re, b_im):
    g, p = lam_re.shape
    rep = lambda a: jnp.repeat(a, SSM_GROUP, axis=1)
    out = [jax.ShapeDtypeStruct((g, p), F32)] * 2 + [jax.ShapeDtypeStruct((g, p * SSM_GROUP), F32)] * 2
    return pl.pallas_call(_s5_prep_kernel, out_shape=out, name="s5_prep")(
        lam_re, lam_im, log_dt.reshape(g, 1), rep(lam_re), rep(lam_im),
        b_re.reshape(g, p * SSM_GROUP), b_im.reshape(g, p * SSM_GROUP))


def _block_diag_in(bb, n_blk, gpb, p):
    v = bb.reshape(n_blk, gpb, p, SSM_GROUP).transpose(0, 1, 3, 2)
    eye = jnp.eye(gpb, dtype=bb.dtype)
    w = v[:, :, :, None, :] * eye[None, :, None, :, None]
    return w.reshape(n_blk, gpb * SSM_GROUP, gpb * p)


def _block_diag_out(c, n_blk, gpb, p):
    v = c.reshape(n_blk, gpb, SSM_GROUP, p).transpose(0, 1, 3, 2)
    eye = jnp.eye(gpb, dtype=c.dtype)
    w = v[:, :, :, None, :] * eye[None, :, None, :, None]
    return w.reshape(n_blk, gpb * p, gpb * SSM_GROUP)


def _complex_step(a_re, a_im, h_re, h_im, b_re, b_im):
    return (a_re * h_re - a_im * h_im + b_re, a_re * h_im + a_im * h_re + b_im)


def _scan_prompt_kernel(x_ref, wb_ref, wcre_ref, wcim_ref, are_ref, aim_ref, d_ref,
                        g_ref, hre_ref, him_ref, s_ref, h_ref, *, n_blk, tb, ns):
    nc = ns // LANES
    block_rows = lambda j: pl.ds(j * BLOCK_PITCH, tb, stride=STEP_PITCH)

    @pl.when(pl.program_id(0) == 0)
    def _():
        h_ref[...] = jnp.zeros_like(h_ref)

    x = x_ref[...]
    for j in range(n_blk):
        u = x[:, j * CH_BLOCK:(j + 1) * CH_BLOCK].astype(BF16)
        bu = _dot(u, wb_ref[j])
        for c in range(2 * nc):
            s_ref[c, block_rows(j), :] = bu[:, c * LANES:(c + 1) * LANES]
    a_re = are_ref[...]
    a_im = aim_ref[...]

    def body(t, carry):
        rows = pl.ds(t * STEP_PITCH, n_blk, stride=BLOCK_PITCH)
        b_re = jnp.concatenate([s_ref[c, rows, :] for c in range(nc)], axis=1)
        b_im = jnp.concatenate([s_ref[nc + c, rows, :] for c in range(nc)], axis=1)
        h_re, h_im = _complex_step(a_re, a_im, carry[0], carry[1], b_re, b_im)
        for c in range(nc):
            s_ref[c, rows, :] = h_re[:, c * LANES:(c + 1) * LANES]
            s_ref[nc + c, rows, :] = h_im[:, c * LANES:(c + 1) * LANES]
        return h_re, h_im

    h_re, h_im = lax.fori_loop(0, tb, body, (h_ref[:, :ns], h_ref[:, ns:]), unroll=8)
    h_ref[...] = jnp.concatenate([h_re, h_im], axis=1)
    hre_ref[...] = h_re
    him_ref[...] = h_im

    ys = []
    for j in range(n_blk):
        hj_re = jnp.concatenate([s_ref[c, block_rows(j), :] for c in range(nc)], axis=1)
        hj_im = jnp.concatenate([s_ref[nc + c, block_rows(j), :] for c in range(nc)], axis=1)
        ys.append(_dot(hj_re.astype(BF16), wcre_ref[j]) - _dot(hj_im.astype(BF16), wcim_ref[j]))
    y = jnp.concatenate(ys, axis=1) + d_ref[...] * x
    g_ref[...] = jax.nn.gelu(y).astype(g_ref.dtype)


def _scan_prompt(x, wb, wc_re, wc_im, a_re, a_im, d, *, tb):
    t, dm = x.shape
    n_blk, _, ns2 = wb.shape
    ns = ns2 // 2
    assert n_blk == SUBLANES and t % tb == 0
    assert (n_blk - 1) * BLOCK_PITCH < STEP_PITCH
    full = lambda shape: pl.BlockSpec(shape, lambda i: (0,) * len(shape))
    const = lambda shape: pl.BlockSpec(shape, lambda i: (0,) * len(shape), pipeline_mode=pl.Buffered(1))
    return pl.pallas_call(
        functools.partial(_scan_prompt_kernel, n_blk=n_blk, tb=tb, ns=ns),
        grid=(t // tb,),
        in_specs=[pl.BlockSpec((tb, dm), lambda i: (i, 0)), const(wb.shape), const(wc_re.shape),
                  const(wc_im.shape), full(a_re.shape), full(a_im.shape), full(d.shape)],
        out_specs=[pl.BlockSpec((tb, dm), lambda i: (i, 0)), full((n_blk, ns)), full((n_blk, ns))],
        out_shape=[jax.ShapeDtypeStruct((t, dm), BF16), jax.ShapeDtypeStruct((n_blk, ns), F32),
                   jax.ShapeDtypeStruct((n_blk, ns), F32)],
        scratch_shapes=[pltpu.VMEM((ns2 // LANES, tb * STEP_PITCH, LANES), F32), pltpu.VMEM((n_blk, ns2), F32)],
        compiler_params=_cparams("arbitrary"),
        name="s5_scan_prompt",
    )(x, wb, wc_re, wc_im, a_re, a_im, d)


def _scan_sample_kernel(x_ref, wb_ref, wcre_ref, wcim_ref, are_ref, aim_ref, d_ref, h0re_ref, h0im_ref,
                        g_ref, hre_ref, him_ref, s_ref, *, n_grp, n_step, ns):
    x = x_ref[...]
    s_ref[...] = _dot(x.astype(BF16), wb_ref[0])
    a_re = jnp.broadcast_to(are_ref[0], (SUBLANES, ns))
    a_im = jnp.broadcast_to(aim_ref[0], (SUBLANES, ns))

    def body(bg, _):
        b0 = pl.multiple_of(bg * SUBLANES, SUBLANES)
        h_re = h0re_ref[pl.ds(b0, SUBLANES), :]
        h_im = h0im_ref[pl.ds(b0, SUBLANES), :]
        for t in range(n_step):
            r0 = pl.multiple_of((bg * n_step + t) * SUBLANES, SUBLANES)
            tile = s_ref[pl.ds(r0, SUBLANES), :]
            h_re, h_im = _complex_step(a_re, a_im, h_re, h_im, tile[:, :ns], tile[:, ns:])
            s_ref[pl.ds(r0, SUBLANES), :] = jnp.concatenate([h_re, h_im], axis=1)
        hre_ref[pl.ds(b0, SUBLANES), :] = h_re
        him_ref[pl.ds(b0, SUBLANES), :] = h_im
        return 0

    lax.fori_loop(0, n_grp, body, 0)
    h = s_ref[...]
    y = _dot(h[:, :ns].astype(BF16), wcre_ref[0]) - _dot(h[:, ns:].astype(BF16), wcim_ref[0])
    g_ref[...] = jax.nn.gelu(y + d_ref[...] * x).astype(g_ref.dtype)


def _scan_sample(x, wb, wc_re, wc_im, a_re, a_im, d, h0_re, h0_im, *, n_step):
    rows, dm = x.shape
    n_blk, _, ns2 = wb.shape
    ns = ns2 // 2
    n_seq = h0_re.shape[0]
    assert rows == n_seq * n_step and n_seq % SUBLANES == 0
    col = lambda shape: pl.BlockSpec(shape, lambda j: (0, j))
    lead = lambda shape: pl.BlockSpec(shape, lambda j: (j,) + (0,) * (len(shape) - 1))
    return pl.pallas_call(
        functools.partial(_scan_sample_kernel, n_grp=n_seq // SUBLANES, n_step=n_step, ns=ns),
        grid=(n_blk,),
        in_specs=[col((rows, CH_BLOCK)), lead((1, CH_BLOCK, ns2)), lead((1, ns, CH_BLOCK)),
                  lead((1, ns, CH_BLOCK)), lead((1, 1, ns)), lead((1, 1, ns)), col((1, CH_BLOCK)),
                  col((n_seq, ns)), col((n_seq, ns))],
        out_specs=[col((rows, CH_BLOCK)), col((n_seq, ns)), col((n_seq, ns))],
        out_shape=[jax.ShapeDtypeStruct((rows, dm), BF16), jax.ShapeDtypeStruct((n_seq, n_blk * ns), F32),
                   jax.ShapeDtypeStruct((n_seq, n_blk * ns), F32)],
        scratch_shapes=[pltpu.VMEM((rows, ns2), F32)],
        compiler_params=_cparams("arbitrary"),
        name="s5_scan_sample",
    )(x, wb, wc_re, wc_im, a_re.reshape(n_blk, 1, ns), a_im.reshape(n_blk, 1, ns), d, h0_re, h0_im)


def _post_norm_store(o_ref, x, sub, g_ref, b_ref):
    o_ref[...] = _layer_norm_rows(DN_ALPHA * x + sub, g_ref[...], b_ref[...])


def _glu_kernel(a_ref, ablk_ref, x_ref, w1_ref, w2_ref, g_ref, b_ref, o_ref, acc_ref):
    f = pl.program_id(1)

    @pl.when(f == 0)
    def _():
        acc_ref[...] = jnp.zeros_like(acc_ref)

    h = ablk_ref[...].astype(F32) * _sigmoid(_dot(a_ref[...], w1_ref[...]))
    acc_ref[...] += _dot(h.astype(BF16), w2_ref[...])

    @pl.when(f == pl.num_programs(1) - 1)
    def _():
        _post_norm_store(o_ref, x_ref[...], acc_ref[...], g_ref, b_ref)


def _mlp_kernel(x_ref, w1_ref, w2_ref, g_ref, b_ref, o_ref, acc_ref):
    f = pl.program_id(1)

    @pl.when(f == 0)
    def _():
        acc_ref[...] = jnp.zeros_like(acc_ref)

    h = jnp.square(jnp.maximum(_dot(x_ref[...].astype(BF16), w1_ref[...]), 0.0))
    acc_ref[...] += _dot(h.astype(BF16), w2_ref[...])

    @pl.when(f == pl.num_programs(1) - 1)
    def _():
        _post_norm_store(o_ref, x_ref[...], acc_ref[...], g_ref, b_ref)


def _two_matmul_specs(tm, dm, tf):
    row = pl.BlockSpec((tm, dm), lambda i, f: (i, 0))
    w1 = pl.BlockSpec((dm, tf), lambda i, f: (0, f))
    w2 = pl.BlockSpec((tf, dm), lambda i, f: (f, 0))
    vec = pl.BlockSpec((1, dm), lambda i, f: (0, 0))
    return row, w1, w2, vec


def _glu(a, x, w1, w2, g, b, *, tm, tf):
    t, dm = x.shape
    row, w1s, w2s, vec = _two_matmul_specs(tm, dm, tf)
    return pl.pallas_call(
        _glu_kernel, grid=(t // tm, w1.shape[1] // tf),
        in_specs=[row, pl.BlockSpec((tm, tf), lambda i, f: (i, f)), row, w1s, w2s, vec, vec],
        out_specs=row, out_shape=jax.ShapeDtypeStruct((t, dm), F32),
        scratch_shapes=[pltpu.VMEM((tm, dm), F32)],
        compiler_params=_cparams("parallel", "arbitrary"), name="glu",
    )(a, a, x, w1, w2, g, b)


def _mlp(x, w1, w2, g, b, *, tm, tf):
    t, dm = x.shape
    row, w1s, w2s, vec = _two_matmul_specs(tm, dm, tf)
    return pl.pallas_call(
        _mlp_kernel, grid=(t // tm, w1.shape[1] // tf),
        in_specs=[row, w1s, w2s, vec, vec],
        out_specs=row, out_shape=jax.ShapeDtypeStruct((t, dm), F32),
        scratch_shapes=[pltpu.VMEM((tm, dm), F32)],
        compiler_params=_cparams("parallel", "arbitrary"), name="mlp",
    )(x, w1, w2, g, b)


def _ple_kernel(x_ref, p_ref, wp_ref, wg_ref, g_ref, b_ref, o_ref):
    x = x_ref[...]
    sub = _dot(p_ref[...].astype(BF16), wp_ref[...]) * _sigmoid(_dot(x.astype(BF16), wg_ref[...]))
    _post_norm_store(o_ref, x, sub, g_ref, b_ref)


def _ple(x, p, wp, wg, g, b, *, tm):
    t, dm = x.shape
    row = pl.BlockSpec((tm, dm), lambda i: (i, 0))
    full = lambda a: pl.BlockSpec(a.shape, lambda i: (0, 0))
    return pl.pallas_call(
        _ple_kernel, grid=(t // tm,),
        in_specs=[row, pl.BlockSpec((tm, p.shape[1]), lambda i: (i, 0)), full(wp), full(wg), full(g), full(b)],
        out_specs=row, out_shape=jax.ShapeDtypeStruct((t, dm), F32),
        compiler_params=_cparams("parallel"), name="ple",
    )(x, p, wp, wg, g, b)


def _attn_out_kernel(o_ref, x_ref, w_ref, g_ref, b_ref, y_ref):
    _post_norm_store(y_ref, x_ref[...], _dot(o_ref[...], w_ref[...]), g_ref, b_ref)


def _attn_out(o, x, w, g, b, *, tm):
    t, dm = x.shape
    row = pl.BlockSpec((tm, dm), lambda i: (i, 0))
    full = lambda a: pl.BlockSpec(a.shape, lambda i: (0, 0))
    return pl.pallas_call(
        _attn_out_kernel, grid=(t // tm,),
        in_specs=[row, row, full(w), full(g), full(b)],
        out_specs=row, out_shape=jax.ShapeDtypeStruct((t, dm), F32),
        compiler_params=_cparams("parallel"), name="attn_out",
    )(o, x, w, g, b)


def _dot_t(w_t, x):
    return lax.dot_general(w_t, x, (((1,), (1,)), ((), ())), preferred_element_type=F32)


def _qkv_prompt_kernel(x_ref, wqt_ref, wk_ref, wv_ref, wvt_ref, qt_ref, k_ref, v_ref, kb_ref, vt_ref):
    xb = x_ref[...].astype(BF16)
    qt_ref[...] = (_dot_t(wqt_ref[...], xb) * (HEAD_DIM ** -0.5)).astype(qt_ref.dtype)
    k = _dot(xb, wk_ref[...])
    k_ref[...] = k
    kb_ref[...] = k.astype(kb_ref.dtype)
    v_ref[...] = _dot(xb, wv_ref[...])
    vt_ref[0] = _dot_t(wvt_ref[...], xb).astype(vt_ref.dtype)


def _qkv_prompt(x, wqt, wk, wv, wvt, *, tm):
    t, dm = x.shape
    dq, dkv = wqt.shape[0], wk.shape[1]
    row = lambda n: pl.BlockSpec((tm, n), lambda i: (i, 0))
    const = lambda a: pl.BlockSpec(a.shape, lambda i: (0, 0), pipeline_mode=pl.Buffered(1))
    return pl.pallas_call(
        _qkv_prompt_kernel, grid=(t // tm,),
        in_specs=[row(dm), const(wqt), const(wk), const(wv), const(wvt)],
        out_specs=[pl.BlockSpec((dq, tm), lambda i: (0, i)), row(dkv), row(dkv), row(dkv),
                   pl.BlockSpec((1, dkv, tm), lambda i: (i, 0, 0))],
        out_shape=[jax.ShapeDtypeStruct((dq, t), BF16), jax.ShapeDtypeStruct((t, dkv), F32),
                   jax.ShapeDtypeStruct((t, dkv), F32), jax.ShapeDtypeStruct((t, dkv), BF16),
                   jax.ShapeDtypeStruct((t // tm, dkv, tm), BF16)],
        compiler_params=_cparams("parallel"), name="qkv_prompt",
    )(x, wqt, wk, wv, wvt)


def _qkv_sample_kernel(x_ref, wq_ref, wk_ref, wv_ref, q_ref, k_ref, v_ref):
    xb = x_ref[...].astype(BF16)
    q_ref[...] = (_dot(xb, wq_ref[...]) * (HEAD_DIM ** -0.5)).astype(q_ref.dtype)
    k_ref[...] = _dot(xb, wk_ref[...])
    v_ref[...] = _dot(xb, wv_ref[...])


def _qkv_sample(x, wq, wk, wv, *, tm):
    t, dm = x.shape
    dq, dkv = wq.shape[1], wk.shape[1]
    row = lambda n: pl.BlockSpec((tm, n), lambda i: (i, 0))
    const = lambda a: pl.BlockSpec(a.shape, lambda i: (0, 0), pipeline_mode=pl.Buffered(1))
    return pl.pallas_call(
        _qkv_sample_kernel, grid=(t // tm,),
        in_specs=[row(dm), const(wq), const(wk), const(wv)],
        out_specs=[row(dq), row(dkv), row(dkv)],
        out_shape=[jax.ShapeDtypeStruct((t, dq), BF16), jax.ShapeDtypeStruct((t, dkv), F32),
                   jax.ShapeDtypeStruct((t, dkv), F32)],
        compiler_params=_cparams("parallel"), name="qkv_sample",
    )(x, wq, wk, wv)


MASKED_LOGIT = -1e30


def _softplus(z):
    neg_abs = lax.bitcast_convert_type(lax.bitcast_convert_type(z, jnp.uint32) | jnp.uint32(0x80000000), F32)
    return jnp.maximum(z, 0.0) + jnp.log(1.0 + jnp.exp(neg_abs))


def _suffix_ones(n):
    return (lax.broadcasted_iota(jnp.int32, (n, n), 0) >= lax.broadcasted_iota(jnp.int32, (n, n), 1)).astype(BF16)


def _suffix_ones_t(n):
    return (lax.broadcasted_iota(jnp.int32, (n, n), 1) >= lax.broadcasted_iota(jnp.int32, (n, n), 0)).astype(BF16)


def _sb_scores(s, bias, mask):
    z = s + bias
    if mask is not None:
        z = jnp.where(mask, z, MASKED_LOGIT)
    sp = _softplus(z)
    hi = sp.astype(BF16)
    return z, hi, (sp - hi.astype(F32)).astype(BF16)


def _sb_weights(z, c):
    return jnp.exp(z - c).astype(BF16)


def _attn_prompt_kernel(bias_ref, qt_ref, k_ref, vt_ref, o_ref, acc_ref, carry_ref,
                        z0_ref, sp0_ref, z1_ref, sp1_ref, *, tq, q_per_kv):
    qi = pl.program_id(1)
    n = q_per_kv * tq
    u2 = jnp.concatenate([_suffix_ones_t(tq)] * 2, axis=1)
    qt = jnp.concatenate([qt_ref[g * HEAD_DIM:(g + 1) * HEAD_DIM, :] for g in range(q_per_kv)], axis=1)
    bias = bias_ref[0]
    acc_ref[...] = jnp.zeros_like(acc_ref)
    carry_ref[...] = jnp.zeros_like(carry_ref)
    stage = ((z0_ref, sp0_ref), (z1_ref, sp1_ref))

    def score(kj, slot, mask):
        z_ref, sp_ref = stage[slot]
        kb = k_ref[pl.ds(pl.multiple_of(kj * tq, tq), tq), :]
        z_ref[...], sp_ref[:tq], sp_ref[tq:] = _sb_scores(_dot(kb, qt), bias, mask)

    def accumulate(kj, slot):
        z_ref, sp_ref = stage[slot]
        c = _dot(u2, sp_ref[...]) + carry_ref[...]
        acc_ref[...] += _dot(vt_ref[kj], _sb_weights(z_ref[...], c))
        carry_ref[...] = c[:1, :]

    key = lax.broadcasted_iota(jnp.int32, (tq, n), 0)
    query = lax.broadcasted_iota(jnp.int32, (tq, n), 1) % tq
    score(qi, 0, key < query)

    def pair(m, _):
        kj = qi - 2 * m
        score(kj - 1, 1, None)
        accumulate(kj, 0)
        score(kj - 2, 0, None)
        accumulate(kj - 1, 1)
        return 0

    lax.fori_loop(0, qi // 2, pair, 0)
    odd = qi % 2

    @pl.when(odd == 1)
    def _():
        accumulate(1, 0)
        score(0, 1, None)
        accumulate(0, 1)

    @pl.when(odd == 0)
    def _():
        accumulate(0, 0)

    o_ref[...] = jnp.concatenate([acc_ref[:, g * tq:(g + 1) * tq].T for g in range(q_per_kv)],
                                 axis=1).astype(o_ref.dtype)


def _attn_prompt(bias_row, qt, kb, vt, *, tq):
    dq, t = qt.shape
    nb = t // tq
    n_kv = kb.shape[1] // HEAD_DIM
    q_per_kv = dq // HEAD_DIM // n_kv
    n = q_per_kv * tq
    assert vt.shape == (nb, n_kv * HEAD_DIM, tq) and bias_row.shape == (n_kv, 1, n)
    return pl.pallas_call(
        functools.partial(_attn_prompt_kernel, tq=tq, q_per_kv=q_per_kv),
        grid=(n_kv, nb),
        in_specs=[pl.BlockSpec((1, 1, n), lambda h, i: (h, 0, 0)),
                  pl.BlockSpec((q_per_kv * HEAD_DIM, tq), lambda h, i: (h, i)),
                  pl.BlockSpec((t, HEAD_DIM), lambda h, i: (0, h)),
                  pl.BlockSpec((nb, HEAD_DIM, tq), lambda h, i: (0, h, 0))],
        out_specs=pl.BlockSpec((tq, q_per_kv * HEAD_DIM), lambda h, i: (i, h)),
        out_shape=jax.ShapeDtypeStruct((t, dq), BF16),
        scratch_shapes=[pltpu.VMEM((HEAD_DIM, n), F32), pltpu.VMEM((1, n), F32)]
        + [pltpu.VMEM((tq, n), F32), pltpu.VMEM((2 * tq, n), BF16)] * 2,
        compiler_params=_cparams("parallel", "arbitrary"), name="attn_prompt",
    )(bias_row, qt, kb, vt)


def _attn_sample_kernel(pt_ref, q_ref, bias_ref, kn_ref, vn_ref, *refs, n_step, n_pages):
    kc_refs, vc_refs, o_ref = refs[:n_pages], refs[n_pages:2 * n_pages], refs[2 * n_pages]
    rows = q_ref.shape[1]
    n_kv = kn_ref.shape[2] // HEAD_DIM
    page = kc_refs[0].shape[1] // n_kv
    rows_per_kv = rows // n_kv
    row_kv = lax.broadcasted_iota(jnp.int32, (rows, HEAD_DIM), 0) // rows_per_kv
    q = q_ref[0]
    q_bd = jnp.concatenate([jnp.where(row_kv == h, q, jnp.zeros_like(q)) for h in range(n_kv)], axis=1)
    bias = bias_ref[...]

    def page_2d(ref):
        heads = [ref[0, pl.ds(h, page, stride=n_kv), :] for h in range(n_kv)]
        return jnp.concatenate(heads, axis=1).astype(BF16)

    def block(acc, carry, k2, v2, u, mask):
        s = lax.dot_general(q_bd, k2, (((1,), (1,)), ((), ())), preferred_element_type=F32)
        z, hi, lo = _sb_scores(s, bias, mask)
        c = _dot(jnp.concatenate([hi, lo], axis=1), jnp.concatenate([u, u], axis=0)) + carry
        return acc + _dot(_sb_weights(z, c), v2), c[:, :1]

    pad = jnp.zeros((page - n_step, n_kv * HEAD_DIM), F32)
    k2 = jnp.concatenate([kn_ref[0], pad], axis=0).astype(BF16)
    v2 = jnp.concatenate([vn_ref[0], pad], axis=0).astype(BF16)
    step = lax.broadcasted_iota(jnp.int32, (rows, page), 0) % n_step
    key = lax.broadcasted_iota(jnp.int32, (rows, page), 1)
    acc, carry = block(jnp.zeros((rows, n_kv * HEAD_DIM), F32), jnp.zeros((rows, 1), F32),
                       k2, v2, _suffix_ones(page), key < step)
    u = _suffix_ones(2 * page)
    for m in reversed(range(n_pages // 2)):
        k2 = jnp.concatenate([page_2d(kc_refs[2 * m]), page_2d(kc_refs[2 * m + 1])], axis=0)
        v2 = jnp.concatenate([page_2d(vc_refs[2 * m]), page_2d(vc_refs[2 * m + 1])], axis=0)
        acc, carry = block(acc, carry, k2, v2, u, None)

    o = jnp.zeros((rows, HEAD_DIM), F32)
    for h in range(n_kv):
        o = o + jnp.where(row_kv == h, acc[:, h * HEAD_DIM:(h + 1) * HEAD_DIM], 0.0)
    o_ref[0] = o.astype(o_ref.dtype)


def _page_index(p, b, pt):
    return (pt[b, p], 0, 0)


def _attn_sample(page_table, q, bias_col, cache_k, cache_v, k_new, v_new):
    n_seq, rows, _ = q.shape
    n_pages = page_table.shape[1]
    _, page, n_kv, _ = cache_k.shape
    n_step = k_new.shape[1]
    assert page % SUBLANES == 0 and n_step <= page and n_pages % 2 == 0
    per_seq = lambda shape: pl.BlockSpec(shape, lambda b, pt: (b,) + (0,) * (len(shape) - 1))
    cache_k = cache_k.reshape(-1, page * n_kv, HEAD_DIM)
    cache_v = cache_v.reshape(-1, page * n_kv, HEAD_DIM)
    pages = [pl.BlockSpec((1, page * n_kv, HEAD_DIM), functools.partial(_page_index, p)) for p in range(n_pages)]
    grid_spec = pltpu.PrefetchScalarGridSpec(
        num_scalar_prefetch=1, grid=(n_seq,),
        in_specs=[per_seq((1, rows, HEAD_DIM)), pl.BlockSpec((rows, 1), lambda b, pt: (0, 0)),
                  per_seq((1, n_step, n_kv * HEAD_DIM)), per_seq((1, n_step, n_kv * HEAD_DIM))] + pages + pages,
        out_specs=per_seq((1, rows, HEAD_DIM)))
    return pl.pallas_call(
        functools.partial(_attn_sample_kernel, n_step=n_step, n_pages=n_pages),
        grid_spec=grid_spec, out_shape=jax.ShapeDtypeStruct((n_seq, rows, HEAD_DIM), BF16),
        compiler_params=_cparams("parallel"), name="attn_sample",
    )(page_table, q, bias_col, k_new, v_new, *([cache_k] * n_pages), *([cache_v] * n_pages))


def kernel(x_prompt, x_sample, p_prompt, p_sample, state_ssm_re, state_ssm_im, cache_k, cache_v, page_table, s5_lam_re, s5_lam_im, s5_log_dt, s5_b_re, s5_b_im, s5_c_re, s5_c_im, s5_d, s5_w_gate, s5_w_out, w_k, w_v, sb_w_q, sb_w_o, sb_bias, ln_mix_g, ln_mix_b, mlp_w1, mlp_w2, ln_mlp_g, ln_mlp_b, ple_w_proj, ple_w_gate, ln_ple_g, ln_ple_b):
    _, t_p, dm = x_prompt.shape
    n_seq, n_step, _ = x_sample.shape
    n_grp_total, p_state = s5_lam_re.shape[1:]
    n_blk = dm // CH_BLOCK
    gpb = CH_BLOCK // SSM_GROUP
    ns = gpb * p_state
    n_heads = sb_w_q.shape[2] // HEAD_DIM
    rows_s = n_seq * n_step
    assert x_prompt.shape[0] == 1 and n_step == SUBLANES and n_seq % SUBLANES == 0
    bf = lambda a: a.astype(BF16)
    vec = lambda a: a.reshape(1, -1)

    def to_group_order(a):
        c = a.shape[-1]
        return a.reshape(n_seq // SUBLANES, SUBLANES, n_step, c).transpose(0, 2, 1, 3).reshape(rows_s, c)

    def to_batch_order(a):
        c = a.shape[-1]
        return a.reshape(n_seq // SUBLANES, n_step, SUBLANES, c).transpose(0, 2, 1, 3).reshape(n_seq, n_step, c)

    xp = x_prompt.reshape(t_p, dm)
    xs = to_group_order(x_sample)
    pp = [p_prompt[i].reshape(t_p, -1) for i in range(DEPTH)]
    ps = [to_group_order(p_sample[i]) for i in range(DEPTH)]

    w_gate, w_out, w_o = bf(s5_w_gate[0]), bf(s5_w_out[0]), bf(sb_w_o[0])
    w1, w2 = [bf(mlp_w1[i]) for i in range(DEPTH)], [bf(mlp_w2[i]) for i in range(DEPTH)]
    wp, wg = [bf(ple_w_proj[i]) for i in range(DEPTH)], [bf(ple_w_gate[i]) for i in range(DEPTH)]
    wq, wk, wv = bf(sb_w_q[0]), bf(w_k), bf(w_v)

    def glu(g, x):
        return _glu(g, x, w_gate, w_out, vec(ln_mix_g[0]), vec(ln_mix_b[0]), tm=512, tf=1024)

    def attn_out(o, x):
        return _attn_out(o, x, w_o, vec(ln_mix_g[1]), vec(ln_mix_b[1]), tm=512)

    def mlp(x, i):
        return _mlp(x, w1[i], w2[i], vec(ln_mlp_g[i]), vec(ln_mlp_b[i]), tm=512, tf=1024)

    def ple(x, p, i):
        return _ple(x, p, wp[i], wg[i], vec(ln_ple_g[i]), vec(ln_ple_b[i]), tm=256)

    ab_re, ab_im, bb_re, bb_im = _s5_prep(s5_lam_re[0], s5_lam_im[0], s5_log_dt[0], s5_b_re[0], s5_b_im[0])
    wb = bf(jnp.concatenate([_block_diag_in(bb_re, n_blk, gpb, p_state),
                             _block_diag_in(bb_im, n_blk, gpb, p_state)], axis=-1))
    wc_re = bf(_block_diag_out(s5_c_re[0], n_blk, gpb, p_state))
    wc_im = bf(_block_diag_out(s5_c_im[0], n_blk, gpb, p_state))
    a_re = ab_re.reshape(n_blk, ns)
    a_im = ab_im.reshape(n_blk, ns)
    d = vec(s5_d[0])
    g_p, hp_re, hp_im = _scan_prompt(xp, wb, wc_re, wc_im, a_re, a_im, d, tb=128)
    g_s, hs_re, hs_im = _scan_sample(xs, wb, wc_re, wc_im, a_re, a_im, d,
                                     state_ssm_re[0].reshape(n_seq, -1), state_ssm_im[0].reshape(n_seq, -1),
                                     n_step=n_step)
    xp = ple(mlp(glu(g_p, xp), 0), pp[0], 0)
    xs = ple(mlp(glu(g_s, xs), 0), ps[0], 0)

    tq = 256
    q_per_kv = n_heads // N_KV_HEADS
    qt, k_p, v_p, kb, vt = _qkv_prompt(xp, wq.T, wk, wv, wv.T, tm=tq)
    bias_row = jnp.repeat(sb_bias[0].reshape(N_KV_HEADS, q_per_kv), tq, axis=1)
    o_p = _attn_prompt(bias_row.reshape(N_KV_HEADS, 1, q_per_kv * tq), qt, kb, vt, tq=tq)
    q_s, k_s, v_s = _qkv_sample(xs, wq, wk, wv, tm=256)
    k_s = to_batch_order(k_s)
    v_s = to_batch_order(v_s)
    q_s = to_batch_order(q_s).reshape(n_seq, n_step, n_heads, HEAD_DIM).transpose(0, 2, 1, 3)
    bias_col = jnp.repeat(sb_bias[0], n_step).reshape(n_heads * n_step, 1)
    o_s = _attn_sample(page_table, q_s.reshape(n_seq, n_heads * n_step, HEAD_DIM), bias_col,
                       cache_k, cache_v, k_s, v_s)
    o_s = o_s.reshape(n_seq, n_heads, n_step, HEAD_DIM).transpose(0, 2, 1, 3).reshape(n_seq, n_step, -1)
    xp = ple(mlp(attn_out(o_p, xp), 1), pp[1], 1)
    xs = ple(mlp(attn_out(to_group_order(o_s), xs), 1), ps[1], 1)

    state = lambda a: a.reshape(1, -1, n_grp_total, p_state)
    kv = lambda a, lead: a.reshape(lead + (N_KV_HEADS, HEAD_DIM))
    return (xp.reshape(1, t_p, dm), to_batch_order(xs),
            state(hp_re), state(hp_im), kv(k_p, (1, t_p)), kv(v_p, (1, t_p)),
            state(hs_re), state(hs_im), kv(k_s, (n_seq, n_step)), kv(v_s, (n_seq, n_step)))
```

```python
import functools
import math

import jax
import jax.numpy as jnp
from jax import lax
from jax.experimental import pallas as pl
from jax.experimental.pallas import tpu as pltpu

F32 = jnp.float32
BF16 = jnp.bfloat16

SSM_GROUP = 16
HEAD_DIM = 128
N_KV_HEADS = 4
LN_EPS = 1e-5
DEPTH = 2
DN_ALPHA = (2.0 * DEPTH) ** 0.25
LOG2E = 1.4426950408889634
QUERY_SCALE = HEAD_DIM ** -0.5 * LOG2E

SUBLANES = 8
LANES = 128
CH_BLOCK = 256
STEP_PITCH = 17
BLOCK_PITCH = 2
VMEM_LIMIT = 56 * 1024 * 1024


def _cparams(*sem):
    return pltpu.CompilerParams(dimension_semantics=sem, vmem_limit_bytes=VMEM_LIMIT)


def _dot(a, b):
    return jnp.dot(a, b, preferred_element_type=F32)


def _sigmoid(x):
    return 1.0 / (1.0 + jnp.exp(-x))


def _layer_norm_rows(y, g, b):
    mu = jnp.mean(y, axis=-1, keepdims=True)
    yc = y - mu
    var = jnp.mean(yc * yc, axis=-1, keepdims=True)
    return yc * lax.rsqrt(var + LN_EPS) * g + b


def _discretize(lam_re, lam_im, dt):
    mag = jnp.exp(lam_re * dt)
    ang = lam_im * dt
    ab_re = mag * jnp.cos(ang)
    ab_im = mag * jnp.sin(ang)
    den = lam_re * lam_re + lam_im * lam_im
    nr = ab_re - 1.0
    f_re = (nr * lam_re + ab_im * lam_im) / den
    f_im = (ab_im * lam_re - nr * lam_im) / den
    return ab_re, ab_im, f_re, f_im


def _s5_prep_kernel(lre_ref, lim_ref, ldt_ref, lre_rep_ref, lim_rep_ref, bre_ref, bim_ref,
                    abre_ref, abim_ref, bbre_ref, bbim_ref):
    dt = jnp.exp(ldt_ref[...])
    ab_re, ab_im, _, _ = _discretize(lre_ref[...], lim_ref[...], dt)
    abre_ref[...] = ab_re
    abim_ref[...] = ab_im
    _, _, f_re, f_im = _discretize(lre_rep_ref[...], lim_rep_ref[...], dt)
    b_re = bre_ref[...]
    b_im = bim_ref[...]
    bbre_ref[...] = f_re * b_re - f_im * b_im
    bbim_ref[...] = f_re * b_im + f_im * b_re


def _s5_prep(lam_re, lam_im, log_dt, b_re, b_im):
    g, p = lam_re.shape
    rep = lambda a: jnp.repeat(a, SSM_GROUP, axis=1)
    out = [jax.ShapeDtypeStruct((g, p), F32)] * 2 + [jax.ShapeDtypeStruct((g, p * SSM_GROUP), F32)] * 2
    return pl.pallas_call(_s5_prep_kernel, out_shape=out, name="s5_prep")(
        lam_re, lam_im, log_dt.reshape(g, 1), rep(lam_re), rep(lam_im),
        b_re.reshape(g, p * SSM_GROUP), b_im.reshape(g, p * SSM_GROUP))


def _block_diag_in(bb, n_blk, gpb, p):
    v = bb.reshape(n_blk, gpb, p, SSM_GROUP).transpose(0, 1, 3, 2)
    eye = jnp.eye(gpb, dtype=bb.dtype)
    w = v[:, :, :, None, :] * eye[None, :, None, :, None]
    return w.reshape(n_blk, gpb * SSM_GROUP, gpb * p)


def _block_diag_out(c, n_blk, gpb, p):
    v = c.reshape(n_blk, gpb, SSM_GROUP, p).transpose(0, 1, 3, 2)
    eye = jnp.eye(gpb, dtype=c.dtype)
    w = v[:, :, :, None, :] * eye[None, :, None, :, None]
    return w.reshape(n_blk, gpb * p, gpb * SSM_GROUP)


def _complex_step(a_re, a_im, h_re, h_im, b_re, b_im):
    return (a_re * h_re - a_im * h_im + b_re, a_re * h_im + a_im * h_re + b_im)


def _scan_prompt_kernel(x_ref, wb_ref, wcre_ref, wcim_ref, are_ref, aim_ref, d_ref,
                        g_ref, hre_ref, him_ref, s_ref, h_ref, *, n_blk, tb, ns):
    nc = ns // LANES
    block_rows = lambda j: pl.ds(j * BLOCK_PITCH, tb, stride=STEP_PITCH)

    @pl.when(pl.program_id(0) == 0)
    def _():
        h_ref[...] = jnp.zeros_like(h_ref)

    x = x_ref[...]
    for j in range(n_blk):
        u = x[:, j * CH_BLOCK:(j + 1) * CH_BLOCK].astype(BF16)
        bu = _dot(u, wb_ref[j])
        for c in range(2 * nc):
            s_ref[c, block_rows(j), :] = bu[:, c * LANES:(c + 1) * LANES]
    a_re = are_ref[...]
    a_im = aim_ref[...]

    def body(t, carry):
        rows = pl.ds(t * STEP_PITCH, n_blk, stride=BLOCK_PITCH)
        b_re = jnp.concatenate([s_ref[c, rows, :] for c in range(nc)], axis=1)
        b_im = jnp.concatenate([s_ref[nc + c, rows, :] for c in range(nc)], axis=1)
        h_re, h_im = _complex_step(a_re, a_im, carry[0], carry[1], b_re, b_im)
        for c in range(nc):
            s_ref[c, rows, :] = h_re[:, c * LANES:(c + 1) * LANES]
            s_ref[nc + c, rows, :] = h_im[:, c * LANES:(c + 1) * LANES]
        return h_re, h_im

    h_re, h_im = lax.fori_loop(0, tb, body, (h_ref[:, :ns], h_ref[:, ns:]), unroll=8)
    h_ref[...] = jnp.concatenate([h_re, h_im], axis=1)
    hre_ref[...] = h_re
    him_ref[...] = h_im

    ys = []
    for j in range(n_blk):
        hj_re = jnp.concatenate([s_ref[c, block_rows(j), :] for c in range(nc)], axis=1)
        hj_im = jnp.concatenate([s_ref[nc + c, block_rows(j), :] for c in range(nc)], axis=1)
        ys.append(_dot(hj_re.astype(BF16), wcre_ref[j]) - _dot(hj_im.astype(BF16), wcim_ref[j]))
    y = jnp.concatenate(ys, axis=1) + d_ref[...] * x
    g_ref[...] = jax.nn.gelu(y).astype(g_ref.dtype)


def _scan_prompt(x, wb, wc_re, wc_im, a_re, a_im, d, *, tb):
    t, dm = x.shape
    n_blk, _, ns2 = wb.shape
    ns = ns2 // 2
    assert n_blk == SUBLANES and t % tb == 0
    assert (n_blk - 1) * BLOCK_PITCH < STEP_PITCH
    full = lambda shape: pl.BlockSpec(shape, lambda i: (0,) * len(shape))
    const = lambda shape: pl.BlockSpec(shape, lambda i: (0,) * len(shape), pipeline_mode=pl.Buffered(1))
    return pl.pallas_call(
        functools.partial(_scan_prompt_kernel, n_blk=n_blk, tb=tb, ns=ns),
        grid=(t // tb,),
        in_specs=[pl.BlockSpec((tb, dm), lambda i: (i, 0)), const(wb.shape), const(wc_re.shape),
                  const(wc_im.shape), full(a_re.shape), full(a_im.shape), full(d.shape)],
        out_specs=[pl.BlockSpec((tb, dm), lambda i: (i, 0)), full((n_blk, ns)), full((n_blk, ns))],
        out_shape=[jax.ShapeDtypeStruct((t, dm), BF16), jax.ShapeDtypeStruct((n_blk, ns), F32),
                   jax.ShapeDtypeStruct((n_blk, ns), F32)],
        scratch_shapes=[pltpu.VMEM((ns2 // LANES, tb * STEP_PITCH, LANES), F32), pltpu.VMEM((n_blk, ns2), F32)],
        compiler_params=_cparams("arbitrary"),
        name="s5_scan_prompt",
    )(x, wb, wc_re, wc_im, a_re, a_im, d)


def _scan_sample_kernel(x_ref, wb_ref, wcre_ref, wcim_ref, are_ref, aim_ref, d_ref, h0re_ref, h0im_ref,
                        g_ref, hre_ref, him_ref, s_ref, *, n_grp, n_step, ns):
    x = x_ref[...]
    s_ref[...] = _dot(x.astype(BF16), wb_ref[0])
    a_re = jnp.broadcast_to(are_ref[0], (SUBLANES, ns))
    a_im = jnp.broadcast_to(aim_ref[0], (SUBLANES, ns))

    def body(bg, _):
        b0 = pl.multiple_of(bg * SUBLANES, SUBLANES)
        h_re = h0re_ref[pl.ds(b0, SUBLANES), :]
        h_im = h0im_ref[pl.ds(b0, SUBLANES), :]
        for t in range(n_step):
            r0 = pl.multiple_of((bg * n_step + t) * SUBLANES, SUBLANES)
            tile = s_ref[pl.ds(r0, SUBLANES), :]
            h_re, h_im = _complex_step(a_re, a_im, h_re, h_im, tile[:, :ns], tile[:, ns:])
            s_ref[pl.ds(r0, SUBLANES), :] = jnp.concatenate([h_re, h_im], axis=1)
        hre_ref[pl.ds(b0, SUBLANES), :] = h_re
        him_ref[pl.ds(b0, SUBLANES), :] = h_im
        return 0

    lax.fori_loop(0, n_grp, body, 0)
    h = s_ref[...]
    y = _dot(h[:, :ns].astype(BF16), wcre_ref[0]) - _dot(h[:, ns:].astype(BF16), wcim_ref[0])
    g_ref[...] = jax.nn.gelu(y + d_ref[...] * x).astype(g_ref.dtype)


def _scan_sample(x, wb, wc_re, wc_im, a_re, a_im, d, h0_re, h0_im, *, n_step):
    rows, dm = x.shape
    n_blk, _, ns2 = wb.shape
    ns = ns2 // 2
    n_seq = h0_re.shape[0]
    assert rows == n_seq * n_step and n_seq % SUBLANES == 0
    col = lambda shape: pl.BlockSpec(shape, lambda j: (0, j))
    lead = lambda shape: pl.BlockSpec(shape, lambda j: (j,) + (0,) * (len(shape) - 1))
    return pl.pallas_call(
        functools.partial(_scan_sample_kernel, n_grp=n_seq // SUBLANES, n_step=n_step, ns=ns),
        grid=(n_blk,),
        in_specs=[col((rows, CH_BLOCK)), lead((1, CH_BLOCK, ns2)), lead((1, ns, CH_BLOCK)),
                  lead((1, ns, CH_BLOCK)), lead((1, 1, ns)), lead((1, 1, ns)), col((1, CH_BLOCK)),
                  col((n_seq, ns)), col((n_seq, ns))],
        out_specs=[col((rows, CH_BLOCK)), col((n_seq, ns)), col((n_seq, ns))],
        out_shape=[jax.ShapeDtypeStruct((rows, dm), BF16), jax.ShapeDtypeStruct((n_seq, n_blk * ns), F32),
                   jax.ShapeDtypeStruct((n_seq, n_blk * ns), F32)],
        scratch_shapes=[pltpu.VMEM((rows, ns2), F32)],
        compiler_params=_cparams("arbitrary"),
        name="s5_scan_sample",
    )(x, wb, wc_re, wc_im, a_re.reshape(n_blk, 1, ns), a_im.reshape(n_blk, 1, ns), d, h0_re, h0_im)


def _post_norm_store(o_ref, x, sub, g_ref, b_ref):
    o_ref[...] = _layer_norm_rows(DN_ALPHA * x + sub, g_ref[...], b_ref[...])


def _glu_kernel(a_ref, ablk_ref, x_ref, w1_ref, w2_ref, g_ref, b_ref, o_ref, acc_ref):
    f = pl.program_id(1)

    @pl.when(f == 0)
    def _():
        acc_ref[...] = jnp.zeros_like(acc_ref)

    h = ablk_ref[...].astype(F32) * _sigmoid(_dot(a_ref[...], w1_ref[...]))
    acc_ref[...] += _dot(h.astype(BF16), w2_ref[...])

    @pl.when(f == pl.num_programs(1) - 1)
    def _():
        _post_norm_store(o_ref, x_ref[...], acc_ref[...], g_ref, b_ref)


def _mlp_kernel(x_ref, w1_ref, w2_ref, g_ref, b_ref, o_ref, acc_ref):
    f = pl.program_id(1)

    @pl.when(f == 0)
    def _():
        acc_ref[...] = jnp.zeros_like(acc_ref)

    h = jnp.square(jnp.maximum(_dot(x_ref[...].astype(BF16), w1_ref[...]), 0.0))
    acc_ref[...] += _dot(h.astype(BF16), w2_ref[...])

    @pl.when(f == pl.num_programs(1) - 1)
    def _():
        _post_norm_store(o_ref, x_ref[...], acc_ref[...], g_ref, b_ref)


def _two_matmul_specs(tm, dm, tf):
    row = pl.BlockSpec((tm, dm), lambda i, f: (i, 0))
    w1 = pl.BlockSpec((dm, tf), lambda i, f: (0, f))
    w2 = pl.BlockSpec((tf, dm), lambda i, f: (f, 0))
    vec = pl.BlockSpec((1, dm), lambda i, f: (0, 0))
    return row, w1, w2, vec


def _glu(a, x, w1, w2, g, b, *, tm, tf):
    t, dm = x.shape
    row, w1s, w2s, vec = _two_matmul_specs(tm, dm, tf)
    return pl.pallas_call(
        _glu_kernel, grid=(t // tm, w1.shape[1] // tf),
        in_specs=[row, pl.BlockSpec((tm, tf), lambda i, f: (i, f)), row, w1s, w2s, vec, vec],
        out_specs=row, out_shape=jax.ShapeDtypeStruct((t, dm), F32),
        scratch_shapes=[pltpu.VMEM((tm, dm), F32)],
        compiler_params=_cparams("parallel", "arbitrary"), name="glu",
    )(a, a, x, w1, w2, g, b)


def _mlp(x, w1, w2, layer, g, b, *, tm, tf):
    t, dm = x.shape
    row, _, _, vec = _two_matmul_specs(tm, dm, tf)
    w1s = pl.BlockSpec((None, dm, tf), lambda i, f: (layer, 0, f))
    w2s = pl.BlockSpec((None, tf, dm), lambda i, f: (layer, f, 0))
    return pl.pallas_call(
        _mlp_kernel, grid=(t // tm, w1.shape[2] // tf),
        in_specs=[row, w1s, w2s, vec, vec],
        out_specs=row, out_shape=jax.ShapeDtypeStruct((t, dm), F32),
        scratch_shapes=[pltpu.VMEM((tm, dm), F32)],
        compiler_params=_cparams("parallel", "arbitrary"), name="mlp",
    )(x, w1, w2, g, b)


def _ple_kernel(x_ref, p_ref, wp_ref, wg_ref, g_ref, b_ref, o_ref):
    x = x_ref[...]
    sub = _dot(p_ref[...].astype(BF16), wp_ref[...]) * _sigmoid(_dot(x.astype(BF16), wg_ref[...]))
    _post_norm_store(o_ref, x, sub, g_ref, b_ref)


def _ple(x, p, wp, wg, layer, g, b, *, tm):
    t, dm = x.shape
    row = pl.BlockSpec((tm, dm), lambda i: (i, 0))
    full = lambda a: pl.BlockSpec(a.shape, lambda i: (0, 0))
    of_layer = lambda a: pl.BlockSpec((None,) + a.shape[1:], lambda i: (layer, 0, 0))
    return pl.pallas_call(
        _ple_kernel, grid=(t // tm,),
        in_specs=[row, pl.BlockSpec((tm, p.shape[1]), lambda i: (i, 0)), of_layer(wp), of_layer(wg),
                  full(g), full(b)],
        out_specs=row, out_shape=jax.ShapeDtypeStruct((t, dm), F32),
        compiler_params=_cparams("parallel"), name="ple",
    )(x, p, wp, wg, g, b)


def _attn_out_kernel(o_ref, x_ref, w_ref, g_ref, b_ref, y_ref):
    _post_norm_store(y_ref, x_ref[...], _dot(o_ref[...], w_ref[...]), g_ref, b_ref)


def _attn_out(o, x, w, g, b, *, tm):
    t, dm = x.shape
    row = pl.BlockSpec((tm, dm), lambda i: (i, 0))
    full = lambda a: pl.BlockSpec(a.shape, lambda i: (0, 0))
    return pl.pallas_call(
        _attn_out_kernel, grid=(t // tm,),
        in_specs=[row, row, full(w), full(g), full(b)],
        out_specs=row, out_shape=jax.ShapeDtypeStruct((t, dm), F32),
        compiler_params=_cparams("parallel"), name="attn_out",
    )(o, x, w, g, b)


def _dot_t(w_t, x):
    return lax.dot_general(w_t, x, (((1,), (1,)), ((), ())), preferred_element_type=F32)


def _qkv_prompt_kernel(x_ref, wqt_ref, wk_ref, wv_ref, wvt_ref, qt_ref, k_ref, v_ref, kb_ref, vt_ref):
    tm = x_ref.shape[0]
    n_kv = wk_ref.shape[1] // HEAD_DIM
    xb = x_ref[...].astype(BF16)
    qt_ref[...] = (_dot_t(wqt_ref[...], xb) * QUERY_SCALE).astype(qt_ref.dtype)
    k = _dot(xb, wk_ref[...])
    v = _dot(xb, wv_ref[...])
    kb_ref[...] = k.astype(kb_ref.dtype)
    vt_ref[0] = _dot_t(wvt_ref[...], xb).astype(vt_ref.dtype)
    for h in range(n_kv):
        k_ref[pl.ds(h, tm, stride=n_kv), :] = k[:, h * HEAD_DIM:(h + 1) * HEAD_DIM]
        v_ref[pl.ds(h, tm, stride=n_kv), :] = v[:, h * HEAD_DIM:(h + 1) * HEAD_DIM]


def _qkv_prompt(x, wqt, wk, wv, wvt, *, tm):
    t, dm = x.shape
    dq, dkv = wqt.shape[0], wk.shape[1]
    n_kv = dkv // HEAD_DIM
    row = lambda n: pl.BlockSpec((tm, n), lambda i: (i, 0))
    head_rows = pl.BlockSpec((tm * n_kv, HEAD_DIM), lambda i: (i, 0))
    const = lambda a: pl.BlockSpec(a.shape, lambda i: (0, 0), pipeline_mode=pl.Buffered(1))
    return pl.pallas_call(
        _qkv_prompt_kernel, grid=(t // tm,),
        in_specs=[row(dm), const(wqt), const(wk), const(wv), const(wvt)],
        out_specs=[pl.BlockSpec((dq, tm), lambda i: (0, i)), head_rows, head_rows, row(dkv),
                   pl.BlockSpec((1, dkv, tm), lambda i: (i, 0, 0))],
        out_shape=[jax.ShapeDtypeStruct((dq, t), BF16), jax.ShapeDtypeStruct((t * n_kv, HEAD_DIM), F32),
                   jax.ShapeDtypeStruct((t * n_kv, HEAD_DIM), F32), jax.ShapeDtypeStruct((t, dkv), BF16),
                   jax.ShapeDtypeStruct((t // tm, dkv, tm), BF16)],
        compiler_params=_cparams("parallel"), name="qkv_prompt",
    )(x, wqt, wk, wv, wvt)


def _qkv_sample_kernel(x_ref, wq_ref, wk_ref, wv_ref, q_ref, k_ref, v_ref):
    xb = x_ref[...].astype(BF16)
    q_ref[...] = (_dot(xb, wq_ref[...]) * QUERY_SCALE).astype(q_ref.dtype)
    k_ref[...] = _dot(xb, wk_ref[...])
    v_ref[...] = _dot(xb, wv_ref[...])


def _qkv_sample(x, wq, wk, wv, *, tm):
    t, dm = x.shape
    dq, dkv = wq.shape[1], wk.shape[1]
    row = lambda n: pl.BlockSpec((tm, n), lambda i: (i, 0))
    const = lambda a: pl.BlockSpec(a.shape, lambda i: (0, 0), pipeline_mode=pl.Buffered(1))
    return pl.pallas_call(
        _qkv_sample_kernel, grid=(t // tm,),
        in_specs=[row(dm), const(wq), const(wk), const(wv)],
        out_specs=[row(dq), row(dkv), row(dkv)],
        out_shape=[jax.ShapeDtypeStruct((t, dq), BF16), jax.ShapeDtypeStruct((t, dkv), F32),
                   jax.ShapeDtypeStruct((t, dkv), F32)],
        compiler_params=_cparams("parallel"), name="qkv_sample",
    )(x, wq, wk, wv)


MASKED_LOGIT = -1e30

EXP2_CLAMP = 126.0


def _softplus2(z):
    return jnp.maximum(jnp.log(1.0 + jnp.exp2(jnp.minimum(z, EXP2_CLAMP))) * LOG2E, z)


def _suffix_ones(n):
    return (lax.broadcasted_iota(jnp.int32, (n, n), 0) >= lax.broadcasted_iota(jnp.int32, (n, n), 1)).astype(BF16)


def _suffix_ones_t(n):
    return (lax.broadcasted_iota(jnp.int32, (n, n), 1) >= lax.broadcasted_iota(jnp.int32, (n, n), 0)).astype(BF16)


def _sb_scores(z, mask):
    if mask is not None:
        z = jnp.where(mask, z, MASKED_LOGIT)
    return z, _softplus2(z).astype(BF16)


def _sb_weights(z, c):
    return jnp.exp2(z - c).astype(BF16)


def _attn_prompt_kernel(bias_ref, qt_ref, k_ref, vt_ref, o_ref, acc_ref, carry_ref,
                        qa_ref, z0_ref, sp0_ref, z1_ref, sp1_ref, *, tq, q_per_kv):
    qi = pl.program_id(1)
    n = q_per_kv * tq
    u = _suffix_ones_t(tq)
    b2 = bias_ref[0] * LOG2E
    b_hi = b2.astype(BF16).astype(F32)
    row = lax.broadcasted_iota(jnp.int32, (HEAD_DIM, n), 0)
    extra = jnp.where(row == 0, b_hi, jnp.where(row == 1, b2 - b_hi, 0.0))
    qa_ref[:HEAD_DIM] = jnp.concatenate([qt_ref[g * HEAD_DIM:(g + 1) * HEAD_DIM, :] for g in range(q_per_kv)],
                                        axis=1)
    qa_ref[HEAD_DIM:] = extra.astype(BF16)
    ones2 = (lax.broadcasted_iota(jnp.int32, (tq, HEAD_DIM), 1) < 2).astype(BF16)
    acc_ref[...] = jnp.zeros_like(acc_ref)
    carry_ref[...] = jnp.zeros_like(carry_ref)
    stage = ((z0_ref, sp0_ref), (z1_ref, sp1_ref))

    def score(kj, slot, mask):
        z_ref, sp_ref = stage[slot]
        kb = k_ref[pl.ds(pl.multiple_of(kj * tq, tq), tq), :]
        ka = jnp.concatenate([kb, ones2], axis=1)
        z_ref[...], sp_ref[...] = _sb_scores(_dot(ka, qa_ref[...]), mask)

    def accumulate(kj, slot):
        z_ref, sp_ref = stage[slot]
        c = _dot(u, sp_ref[...]) + carry_ref[...]
        acc_ref[...] += _dot(vt_ref[kj], _sb_weights(z_ref[...], c))
        carry_ref[...] = c[:1, :]

    key = lax.broadcasted_iota(jnp.int32, (tq, n), 0)
    query = lax.broadcasted_iota(jnp.int32, (tq, n), 1) % tq
    score(qi, 0, key < query)

    def step(kj, slot):
        z_ref, sp_ref = stage[slot]
        zn_ref, spn_ref = stage[1 - slot]
        kb = k_ref[pl.ds(pl.multiple_of((kj - 1) * tq, tq), tq), :]
        s = _dot(jnp.concatenate([kb, ones2], axis=1), qa_ref[...])
        c = _dot(u, sp_ref[...]) + carry_ref[...]
        zn_ref[...], spn_ref[...] = _sb_scores(s, None)
        acc_ref[...] += _dot(vt_ref[kj], _sb_weights(z_ref[...], c))
        carry_ref[...] = c[:1, :]

    def pair(m, _):
        kj = qi - 2 * m
        step(kj, 0)
        step(kj - 1, 1)
        return 0

    lax.fori_loop(0, qi // 2, pair, 0)
    odd = qi % 2

    @pl.when(odd == 1)
    def _():
        step(1, 0)
        accumulate(0, 1)

    @pl.when(odd == 0)
    def _():
        accumulate(0, 0)

    o_ref[...] = jnp.concatenate([acc_ref[:, g * tq:(g + 1) * tq].T for g in range(q_per_kv)],
                                 axis=1).astype(o_ref.dtype)


def _attn_prompt(bias_row, qt, kb, vt, *, tq):
    dq, t = qt.shape
    nb = t // tq
    n_kv = kb.shape[1] // HEAD_DIM
    q_per_kv = dq // HEAD_DIM // n_kv
    n = q_per_kv * tq
    assert vt.shape == (nb, n_kv * HEAD_DIM, tq) and bias_row.shape == (n_kv, 1, n)
    return pl.pallas_call(
        functools.partial(_attn_prompt_kernel, tq=tq, q_per_kv=q_per_kv),
        grid=(n_kv, nb),
        in_specs=[pl.BlockSpec((1, 1, n), lambda h, i: (h, 0, 0)),
                  pl.BlockSpec((q_per_kv * HEAD_DIM, tq), lambda h, i: (h, i)),
                  pl.BlockSpec((t, HEAD_DIM), lambda h, i: (0, h)),
                  pl.BlockSpec((nb, HEAD_DIM, tq), lambda h, i: (0, h, 0))],
        out_specs=pl.BlockSpec((tq, q_per_kv * HEAD_DIM), lambda h, i: (i, h)),
        out_shape=jax.ShapeDtypeStruct((t, dq), BF16),
        scratch_shapes=[pltpu.VMEM((HEAD_DIM, n), F32), pltpu.VMEM((1, n), F32)]
        + [pltpu.VMEM((2 * HEAD_DIM, n), BF16)] + [pltpu.VMEM((tq, n), F32), pltpu.VMEM((tq, n), BF16)] * 2,
        compiler_params=_cparams("parallel", "arbitrary"), name="attn_prompt",
    )(bias_row, qt, kb, vt)


def _attn_sample_kernel(pt_ref, q_ref, bias_ref, kn_ref, vn_ref, *refs, n_step, n_pages):
    kc_refs, vc_refs, o_ref = refs[:n_pages], refs[n_pages:2 * n_pages], refs[2 * n_pages]
    rows = q_ref.shape[1]
    n_kv = kn_ref.shape[2] // HEAD_DIM
    page = kc_refs[0].shape[1] // n_kv
    rows_per_kv = rows // n_kv
    row_kv = lax.broadcasted_iota(jnp.int32, (rows, HEAD_DIM), 0) // rows_per_kv
    q = q_ref[0]
    q_bd = jnp.concatenate([jnp.where(row_kv == h, q, jnp.zeros_like(q)) for h in range(n_kv)], axis=1)
    bias = bias_ref[...] * LOG2E

    def page_2d(ref):
        heads = [ref[0, pl.ds(h, page, stride=n_kv), :] for h in range(n_kv)]
        return jnp.concatenate(heads, axis=1).astype(BF16)

    def new_tokens(ref):
        pad = jnp.zeros((page - n_step, n_kv * HEAD_DIM), F32)
        return jnp.concatenate([ref[0], pad], axis=0).astype(BF16)

    def page_pair(refs, m):
        return jnp.concatenate([page_2d(refs[2 * m]), page_2d(refs[2 * m + 1])], axis=0)

    def score(k2, mask):
        s = lax.dot_general(q_bd, k2, (((1,), (1,)), ((), ())), preferred_element_type=F32)
        return _sb_scores(s + bias, mask)

    step = lax.broadcasted_iota(jnp.int32, (rows, page), 0) % n_step
    key = lax.broadcasted_iota(jnp.int32, (rows, page), 1)
    pairs = list(reversed(range(n_pages // 2)))
    u1, u2 = _suffix_ones(page), _suffix_ones(2 * page)
    scored = [score(new_tokens(kn_ref), key < step)] + [score(page_pair(kc_refs, m), None) for m in pairs]
    sums = [_dot(sp, u1 if i == 0 else u2) for i, (_, sp) in enumerate(scored)]
    values = [new_tokens(vn_ref)] + [page_pair(vc_refs, m) for m in pairs]
    acc = jnp.zeros((rows, n_kv * HEAD_DIM), F32)
    carry = jnp.zeros((rows, 1), F32)
    for (z, _), c_local, v2 in zip(scored, sums, values):
        c = c_local + carry
        acc = acc + _dot(_sb_weights(z, c), v2)
        carry = c[:, :1]

    o = jnp.zeros((rows, HEAD_DIM), F32)
    for h in range(n_kv):
        o = o + jnp.where(row_kv == h, acc[:, h * HEAD_DIM:(h + 1) * HEAD_DIM], 0.0)
    o_ref[0] = o.astype(o_ref.dtype)


def _page_index(p, b, pt):
    return (pt[b, p], 0, 0)


def _attn_sample(page_table, q, bias_col, cache_k, cache_v, k_new, v_new):
    n_seq, rows, _ = q.shape
    n_pages = page_table.shape[1]
    _, page, n_kv, _ = cache_k.shape
    n_step = k_new.shape[1]
    assert page % SUBLANES == 0 and n_step <= page and n_pages % 2 == 0
    per_seq = lambda shape: pl.BlockSpec(shape, lambda b, pt: (b,) + (0,) * (len(shape) - 1))
    cache_k = cache_k.reshape(-1, page * n_kv, HEAD_DIM)
    cache_v = cache_v.reshape(-1, page * n_kv, HEAD_DIM)
    pages = [pl.BlockSpec((1, page * n_kv, HEAD_DIM), functools.partial(_page_index, p)) for p in range(n_pages)]
    grid_spec = pltpu.PrefetchScalarGridSpec(
        num_scalar_prefetch=1, grid=(n_seq,),
        in_specs=[per_seq((1, rows, HEAD_DIM)), pl.BlockSpec((rows, 1), lambda b, pt: (0, 0)),
                  per_seq((1, n_step, n_kv * HEAD_DIM)), per_seq((1, n_step, n_kv * HEAD_DIM))] + pages + pages,
        out_specs=per_seq((1, rows, HEAD_DIM)))
    return pl.pallas_call(
        functools.partial(_attn_sample_kernel, n_step=n_step, n_pages=n_pages),
        grid_spec=grid_spec, out_shape=jax.ShapeDtypeStruct((n_seq, rows, HEAD_DIM), BF16),
        compiler_params=_cparams("parallel"), name="attn_sample",
    )(page_table, q, bias_col, k_new, v_new, *([cache_k] * n_pages), *([cache_v] * n_pages))


def kernel(x_prompt, x_sample, p_prompt, p_sample, state_ssm_re, state_ssm_im, cache_k, cache_v, page_table, s5_lam_re, s5_lam_im, s5_log_dt, s5_b_re, s5_b_im, s5_c_re, s5_c_im, s5_d, s5_w_gate, s5_w_out, w_k, w_v, sb_w_q, sb_w_o, sb_bias, ln_mix_g, ln_mix_b, mlp_w1, mlp_w2, ln_mlp_g, ln_mlp_b, ple_w_proj, ple_w_gate, ln_ple_g, ln_ple_b):
    _, t_p, dm = x_prompt.shape
    n_seq, n_step, _ = x_sample.shape
    n_grp_total, p_state = s5_lam_re.shape[1:]
    n_blk = dm // CH_BLOCK
    gpb = CH_BLOCK // SSM_GROUP
    ns = gpb * p_state
    n_heads = sb_w_q.shape[2] // HEAD_DIM
    rows_s = n_seq * n_step
    assert x_prompt.shape[0] == 1 and n_step == SUBLANES and n_seq % SUBLANES == 0
    bf = lambda a: a.astype(BF16)
    vec = lambda a: a.reshape(1, -1)

    def to_group_order(a):
        c = a.shape[-1]
        return a.reshape(n_seq // SUBLANES, SUBLANES, n_step, c).transpose(0, 2, 1, 3).reshape(rows_s, c)

    def to_batch_order(a):
        c = a.shape[-1]
        return a.reshape(n_seq // SUBLANES, n_step, SUBLANES, c).transpose(0, 2, 1, 3).reshape(n_seq, n_step, c)

    xp = x_prompt.reshape(t_p, dm)
    xs = to_group_order(x_sample)
    pp = [p_prompt[i].reshape(t_p, -1) for i in range(DEPTH)]
    ps = [to_group_order(p_sample[i]) for i in range(DEPTH)]

    w_gate, w_out, w_o = bf(s5_w_gate[0]), bf(s5_w_out[0]), bf(sb_w_o[0])
    w1, w2, wp, wg = bf(mlp_w1), bf(mlp_w2), bf(ple_w_proj), bf(ple_w_gate)
    wq, wk, wv = bf(sb_w_q[0]), bf(w_k), bf(w_v)

    def glu(g, x):
        return _glu(g, x, w_gate, w_out, vec(ln_mix_g[0]), vec(ln_mix_b[0]), tm=512, tf=1024)

    def attn_out(o, x):
        return _attn_out(o, x, w_o, vec(ln_mix_g[1]), vec(ln_mix_b[1]), tm=512)

    def mlp(x, i):
        return _mlp(x, w1, w2, i, vec(ln_mlp_g[i]), vec(ln_mlp_b[i]), tm=512, tf=1024)

    def ple(x, p, i):
        return _ple(x, p, wp, wg, i, vec(ln_ple_g[i]), vec(ln_ple_b[i]), tm=256)

    ab_re, ab_im, bb_re, bb_im = _s5_prep(s5_lam_re[0], s5_lam_im[0], s5_log_dt[0], s5_b_re[0], s5_b_im[0])
    wb = bf(jnp.concatenate([_block_diag_in(bb_re, n_blk, gpb, p_state),
                             _block_diag_in(bb_im, n_blk, gpb, p_state)], axis=-1))
    wc_re = bf(_block_diag_out(s5_c_re[0], n_blk, gpb, p_state))
    wc_im = bf(_block_diag_out(s5_c_im[0], n_blk, gpb, p_state))
    a_re = ab_re.reshape(n_blk, ns)
    a_im = ab_im.reshape(n_blk, ns)
    d = vec(s5_d[0])
    g_p, hp_re, hp_im = _scan_prompt(xp, wb, wc_re, wc_im, a_re, a_im, d, tb=128)
    g_s, hs_re, hs_im = _scan_sample(xs, wb, wc_re, wc_im, a_re, a_im, d,
                                     state_ssm_re[0].reshape(n_seq, -1), state_ssm_im[0].reshape(n_seq, -1),
                                     n_step=n_step)
    xp = ple(mlp(glu(g_p, xp), 0), pp[0], 0)
    xs = ple(mlp(glu(g_s, xs), 0), ps[0], 0)

    tq = 256
    q_per_kv = n_heads // N_KV_HEADS
    qt, k_p, v_p, kb, vt = _qkv_prompt(xp, wq.T, wk, wv, wv.T, tm=tq)
    bias_row = jnp.repeat(sb_bias[0].reshape(N_KV_HEADS, q_per_kv), tq, axis=1)
    o_p = _attn_prompt(bias_row.reshape(N_KV_HEADS, 1, q_per_kv * tq), qt, kb, vt, tq=tq)
    q_s, k_s, v_s = _qkv_sample(xs, wq, wk, wv, tm=256)
    k_s = to_batch_order(k_s)
    v_s = to_batch_order(v_s)
    q_s = to_batch_order(q_s).reshape(n_seq, n_step, n_heads, HEAD_DIM).transpose(0, 2, 1, 3)
    bias_col = jnp.repeat(sb_bias[0], n_step).reshape(n_heads * n_step, 1)
    o_s = _attn_sample(page_table, q_s.reshape(n_seq, n_heads * n_step, HEAD_DIM), bias_col,
                       cache_k, cache_v, k_s, v_s)
    o_s = o_s.reshape(n_seq, n_heads, n_step, HEAD_DIM).transpose(0, 2, 1, 3).reshape(n_seq, n_step, -1)
    xp = ple(mlp(attn_out(o_p, xp), 1), pp[1], 1)
    xs = ple(mlp(attn_out(to_group_order(o_s), xs), 1), ps[1], 1)

    state = lambda a: a.reshape(1, -1, n_grp_total, p_state)
    kv = lambda a, lead: a.reshape(lead + (N_KV_HEADS, HEAD_DIM))
    return (xp.reshape(1, t_p, dm), to_batch_order(xs),
            state(hp_re), state(hp_im), kv(k_p, (1, t_p)), kv(v_p, (1, t_p)),
            state(hs_re), state(hs_im), kv(k_s, (n_seq, n_step)), kv(v_s, (n_seq, n_step)))
```

```python
import functools
import math

import jax
import jax.numpy as jnp
from jax import lax
from jax.experimental import pallas as pl
from jax.experimental.pallas import tpu as pltpu

F32 = jnp.float32
BF16 = jnp.bfloat16

SSM_GROUP = 16
HEAD_DIM = 128
N_KV_HEADS = 4
LN_EPS = 1e-5
DEPTH = 2
DN_ALPHA = (2.0 * DEPTH) ** 0.25
LOG2E = 1.4426950408889634
QUERY_SCALE = HEAD_DIM ** -0.5 * LOG2E

SUBLANES = 8
LANES = 128
CH_BLOCK = 256
STEP_PITCH = 17
BLOCK_PITCH = 2
VMEM_LIMIT = 56 * 1024 * 1024


def _cparams(*sem):
    return pltpu.CompilerParams(dimension_semantics=sem, vmem_limit_bytes=VMEM_LIMIT)


def _dot(a, b):
    return jnp.dot(a, b, preferred_element_type=F32)


def _sigmoid(x):
    return 1.0 / (1.0 + jnp.exp(-x))


def _layer_norm_rows(y, g, b):
    mu = jnp.mean(y, axis=-1, keepdims=True)
    yc = y - mu
    var = jnp.mean(yc * yc, axis=-1, keepdims=True)
    return yc * lax.rsqrt(var + LN_EPS) * g + b


def _discretize(lam_re, lam_im, dt):
    mag = jnp.exp(lam_re * dt)
    ang = lam_im * dt
    ab_re = mag * jnp.cos(ang)
    ab_im = mag * jnp.sin(ang)
    den = lam_re * lam_re + lam_im * lam_im
    nr = ab_re - 1.0
    f_re = (nr * lam_re + ab_im * lam_im) / den
    f_im = (ab_im * lam_re - nr * lam_im) / den
    return ab_re, ab_im, f_re, f_im


def _s5_prep_kernel(lre_ref, lim_ref, ldt_ref, lre_rep_ref, lim_rep_ref, bre_ref, bim_ref,
                    abre_ref, abim_ref, bbre_ref, bbim_ref):
    dt = jnp.exp(ldt_ref[...])
    ab_re, ab_im, _, _ = _discretize(lre_ref[...], lim_ref[...], dt)
    abre_ref[...] = ab_re
    abim_ref[...] = ab_im
    _, _, f_re, f_im = _discretize(lre_rep_ref[...], lim_rep_ref[...], dt)
    b_re = bre_ref[...]
    b_im = bim_ref[...]
    bbre_ref[...] = f_re * b_re - f_im * b_im
    bbim_ref[...] = f_re * b_im + f_im * b_re


def _s5_prep(lam_re, lam_im, log_dt, b_re, b_im):
    g, p = lam_re.shape
    rep = lambda a: jnp.repeat(a, SSM_GROUP, axis=1)
    out = [jax.ShapeDtypeStruct((g, p), F32)] * 2 + [jax.ShapeDtypeStruct((g, p * SSM_GROUP), F32)] * 2
    return pl.pallas_call(_s5_prep_kernel, out_shape=out, name="s5_prep")(
        lam_re, lam_im, log_dt.reshape(g, 1), rep(lam_re), rep(lam_im),
        b_re.reshape(g, p * SSM_GROUP), b_im.reshape(g, p * SSM_GROUP))


def _block_diag(v, width):
    n_blk, rows, _ = v.shape
    gpb = CH_BLOCK // SSM_GROUP
    tiled = jnp.tile(v, (1, 1, gpb))
    row_grp = lax.broadcasted_iota(jnp.int32, (rows, gpb * width), 0) // (rows // gpb)
    col_grp = lax.broadcasted_iota(jnp.int32, (rows, gpb * width), 1) // width
    return jnp.where((row_grp == col_grp)[None], tiled, 0.0).astype(BF16)


def _block_diag_in(bb, n_blk, gpb, p):
    v = bb.reshape(n_blk, gpb, p, SSM_GROUP).transpose(0, 1, 3, 2)
    return _block_diag(v.reshape(n_blk, gpb * SSM_GROUP, p), p)


def _block_diag_out(c, n_blk, gpb, p):
    v = c.reshape(n_blk, gpb, SSM_GROUP, p).transpose(0, 1, 3, 2)
    return _block_diag(v.reshape(n_blk, gpb * p, SSM_GROUP), SSM_GROUP)


def _complex_step(a_re, a_im, h_re, h_im, b_re, b_im):
    return (a_re * h_re - a_im * h_im + b_re, a_re * h_im + a_im * h_re + b_im)


def _scan_prompt_kernel(x_ref, wb_ref, wcre_ref, wcim_ref, are_ref, aim_ref, d_ref,
                        g_ref, hre_ref, him_ref, s_ref, h_ref, *, n_blk, tb, ns):
    nc = ns // LANES
    block_rows = lambda j: pl.ds(j * BLOCK_PITCH, tb, stride=STEP_PITCH)

    @pl.when(pl.program_id(0) == 0)
    def _():
        h_ref[...] = jnp.zeros_like(h_ref)

    x = x_ref[...]
    for j in range(n_blk):
        u = x[:, j * CH_BLOCK:(j + 1) * CH_BLOCK].astype(BF16)
        bu = _dot(u, wb_ref[j])
        for c in range(2 * nc):
            s_ref[c, block_rows(j), :] = bu[:, c * LANES:(c + 1) * LANES]
    a_re = are_ref[...]
    a_im = aim_ref[...]

    def body(t, carry):
        rows = pl.ds(t * STEP_PITCH, n_blk, stride=BLOCK_PITCH)
        b_re = jnp.concatenate([s_ref[c, rows, :] for c in range(nc)], axis=1)
        b_im = jnp.concatenate([s_ref[nc + c, rows, :] for c in range(nc)], axis=1)
        h_re, h_im = _complex_step(a_re, a_im, carry[0], carry[1], b_re, b_im)
        for c in range(nc):
            s_ref[c, rows, :] = h_re[:, c * LANES:(c + 1) * LANES]
            s_ref[nc + c, rows, :] = h_im[:, c * LANES:(c + 1) * LANES]
        return h_re, h_im

    h_re, h_im = lax.fori_loop(0, tb, body, (h_ref[:, :ns], h_ref[:, ns:]), unroll=8)
    h_ref[...] = jnp.concatenate([h_re, h_im], axis=1)
    hre_ref[...] = h_re
    him_ref[...] = h_im

    ys = []
    for j in range(n_blk):
        hj_re = jnp.concatenate([s_ref[c, block_rows(j), :] for c in range(nc)], axis=1)
        hj_im = jnp.concatenate([s_ref[nc + c, block_rows(j), :] for c in range(nc)], axis=1)
        ys.append(_dot(hj_re.astype(BF16), wcre_ref[j]) - _dot(hj_im.astype(BF16), wcim_ref[j]))
    y = jnp.concatenate(ys, axis=1) + d_ref[...] * x
    g_ref[...] = jax.nn.gelu(y).astype(g_ref.dtype)


def _scan_prompt(x, wb, wc_re, wc_im, a_re, a_im, d, *, tb):
    t, dm = x.shape
    n_blk, _, ns2 = wb.shape
    ns = ns2 // 2
    assert n_blk == SUBLANES and t % tb == 0
    assert (n_blk - 1) * BLOCK_PITCH < STEP_PITCH
    full = lambda shape: pl.BlockSpec(shape, lambda i: (0,) * len(shape))
    const = lambda shape: pl.BlockSpec(shape, lambda i: (0,) * len(shape), pipeline_mode=pl.Buffered(1))
    return pl.pallas_call(
        functools.partial(_scan_prompt_kernel, n_blk=n_blk, tb=tb, ns=ns),
        grid=(t // tb,),
        in_specs=[pl.BlockSpec((tb, dm), lambda i: (i, 0)), const(wb.shape), const(wc_re.shape),
                  const(wc_im.shape), full(a_re.shape), full(a_im.shape), full(d.shape)],
        out_specs=[pl.BlockSpec((tb, dm), lambda i: (i, 0)), full((n_blk, ns)), full((n_blk, ns))],
        out_shape=[jax.ShapeDtypeStruct((t, dm), BF16), jax.ShapeDtypeStruct((n_blk, ns), F32),
                   jax.ShapeDtypeStruct((n_blk, ns), F32)],
        scratch_shapes=[pltpu.VMEM((ns2 // LANES, tb * STEP_PITCH, LANES), F32), pltpu.VMEM((n_blk, ns2), F32)],
        compiler_params=_cparams("arbitrary"),
        name="s5_scan_prompt",
    )(x, wb, wc_re, wc_im, a_re, a_im, d)


def _scan_sample_kernel(x_ref, wb_ref, wcre_ref, wcim_ref, are_ref, aim_ref, d_ref, h0re_ref, h0im_ref,
                        g_ref, hre_ref, him_ref, s_ref, *, n_grp, n_step, ns):
    x = x_ref[...]
    s_ref[...] = _dot(x.astype(BF16), wb_ref[0])
    a_re = jnp.broadcast_to(are_ref[0], (SUBLANES, ns))
    a_im = jnp.broadcast_to(aim_ref[0], (SUBLANES, ns))

    def body(bg, _):
        b0 = pl.multiple_of(bg * SUBLANES, SUBLANES)
        h_re = h0re_ref[pl.ds(b0, SUBLANES), :]
        h_im = h0im_ref[pl.ds(b0, SUBLANES), :]
        for t in range(n_step):
            r0 = pl.multiple_of((bg * n_step + t) * SUBLANES, SUBLANES)
            tile = s_ref[pl.ds(r0, SUBLANES), :]
            h_re, h_im = _complex_step(a_re, a_im, h_re, h_im, tile[:, :ns], tile[:, ns:])
            s_ref[pl.ds(r0, SUBLANES), :] = jnp.concatenate([h_re, h_im], axis=1)
        hre_ref[pl.ds(b0, SUBLANES), :] = h_re
        him_ref[pl.ds(b0, SUBLANES), :] = h_im
        return 0

    lax.fori_loop(0, n_grp, body, 0)
    h = s_ref[...]
    y = _dot(h[:, :ns].astype(BF16), wcre_ref[0]) - _dot(h[:, ns:].astype(BF16), wcim_ref[0])
    g_ref[...] = jax.nn.gelu(y + d_ref[...] * x).astype(g_ref.dtype)


def _scan_sample(x, wb, wc_re, wc_im, a_re, a_im, d, h0_re, h0_im, *, n_step):
    rows, dm = x.shape
    n_blk, _, ns2 = wb.shape
    ns = ns2 // 2
    n_seq = h0_re.shape[0]
    assert rows == n_seq * n_step and n_seq % SUBLANES == 0
    col = lambda shape: pl.BlockSpec(shape, lambda j: (0, j))
    lead = lambda shape: pl.BlockSpec(shape, lambda j: (j,) + (0,) * (len(shape) - 1))
    return pl.pallas_call(
        functools.partial(_scan_sample_kernel, n_grp=n_seq // SUBLANES, n_step=n_step, ns=ns),
        grid=(n_blk,),
        in_specs=[col((rows, CH_BLOCK)), lead((1, CH_BLOCK, ns2)), lead((1, ns, CH_BLOCK)),
                  lead((1, ns, CH_BLOCK)), lead((1, 1, ns)), lead((1, 1, ns)), col((1, CH_BLOCK)),
                  col((n_seq, ns)), col((n_seq, ns))],
        out_specs=[col((rows, CH_BLOCK)), col((n_seq, ns)), col((n_seq, ns))],
        out_shape=[jax.ShapeDtypeStruct((rows, dm), BF16), jax.ShapeDtypeStruct((n_seq, n_blk * ns), F32),
                   jax.ShapeDtypeStruct((n_seq, n_blk * ns), F32)],
        scratch_shapes=[pltpu.VMEM((rows, ns2), F32)],
        compiler_params=_cparams("arbitrary"),
        name="s5_scan_sample",
    )(x, wb, wc_re, wc_im, a_re.reshape(n_blk, 1, ns), a_im.reshape(n_blk, 1, ns), d, h0_re, h0_im)


def _post_norm_store(o_ref, x, sub, g_ref, b_ref):
    o_ref[...] = _layer_norm_rows(DN_ALPHA * x + sub, g_ref[...], b_ref[...])


def _glu_kernel(a_ref, ablk_ref, x_ref, w1_ref, w2_ref, g_ref, b_ref, o_ref, acc_ref):
    f = pl.program_id(1)

    @pl.when(f == 0)
    def _():
        acc_ref[...] = jnp.zeros_like(acc_ref)

    h = ablk_ref[...].astype(F32) * _sigmoid(_dot(a_ref[...], w1_ref[...]))
    acc_ref[...] += _dot(h.astype(BF16), w2_ref[...])

    @pl.when(f == pl.num_programs(1) - 1)
    def _():
        _post_norm_store(o_ref, x_ref[...], acc_ref[...], g_ref, b_ref)


def _mlp_kernel(x_ref, w1_ref, w2_ref, g_ref, b_ref, o_ref, acc_ref):
    f = pl.program_id(1)

    @pl.when(f == 0)
    def _():
        acc_ref[...] = jnp.zeros_like(acc_ref)

    h = jnp.square(jnp.maximum(_dot(x_ref[...].astype(BF16), w1_ref[...]), 0.0))
    acc_ref[...] += _dot(h.astype(BF16), w2_ref[...])

    @pl.when(f == pl.num_programs(1) - 1)
    def _():
        _post_norm_store(o_ref, x_ref[...], acc_ref[...], g_ref, b_ref)


def _two_matmul_specs(tm, dm, tf):
    row = pl.BlockSpec((tm, dm), lambda i, f: (i, 0))
    w1 = pl.BlockSpec((dm, tf), lambda i, f: (0, f))
    w2 = pl.BlockSpec((tf, dm), lambda i, f: (f, 0))
    vec = pl.BlockSpec((1, dm), lambda i, f: (0, 0))
    return row, w1, w2, vec


def _glu(a, x, w1, w2, g, b, *, tm, tf):
    t, dm = x.shape
    row, w1s, w2s, vec = _two_matmul_specs(tm, dm, tf)
    return pl.pallas_call(
        _glu_kernel, grid=(t // tm, w1.shape[1] // tf),
        in_specs=[row, pl.BlockSpec((tm, tf), lambda i, f: (i, f)), row, w1s, w2s, vec, vec],
        out_specs=row, out_shape=jax.ShapeDtypeStruct((t, dm), F32),
        scratch_shapes=[pltpu.VMEM((tm, dm), F32)],
        compiler_params=_cparams("parallel", "arbitrary"), name="glu",
    )(a, a, x, w1, w2, g, b)


def _mlp(x, w1, w2, layer, g, b, *, tm, tf):
    t, dm = x.shape
    row, _, _, vec = _two_matmul_specs(tm, dm, tf)
    w1s = pl.BlockSpec((None, dm, tf), lambda i, f: (layer, 0, f))
    w2s = pl.BlockSpec((None, tf, dm), lambda i, f: (layer, f, 0))
    return pl.pallas_call(
        _mlp_kernel, grid=(t // tm, w1.shape[2] // tf),
        in_specs=[row, w1s, w2s, vec, vec],
        out_specs=row, out_shape=jax.ShapeDtypeStruct((t, dm), F32),
        scratch_shapes=[pltpu.VMEM((tm, dm), F32)],
        compiler_params=_cparams("parallel", "arbitrary"), name="mlp",
    )(x, w1, w2, g, b)


def _ple_kernel(x_ref, p_ref, wp_ref, wg_ref, g_ref, b_ref, o_ref):
    x = x_ref[...]
    sub = _dot(p_ref[...].astype(BF16), wp_ref[...]) * _sigmoid(_dot(x.astype(BF16), wg_ref[...]))
    _post_norm_store(o_ref, x, sub, g_ref, b_ref)


def _ple(x, p, wp, wg, layer, g, b, *, tm):
    t, dm = x.shape
    row = pl.BlockSpec((tm, dm), lambda i: (i, 0))
    full = lambda a: pl.BlockSpec(a.shape, lambda i: (0, 0))
    of_layer = lambda a: pl.BlockSpec((None,) + a.shape[1:], lambda i: (layer, 0, 0),
                                      pipeline_mode=pl.Buffered(1))
    return pl.pallas_call(
        _ple_kernel, grid=(t // tm,),
        in_specs=[row, pl.BlockSpec((tm, p.shape[1]), lambda i: (i, 0)), of_layer(wp), of_layer(wg),
                  full(g), full(b)],
        out_specs=row, out_shape=jax.ShapeDtypeStruct((t, dm), F32),
        compiler_params=_cparams("parallel"), name="ple",
    )(x, p, wp, wg, g, b)


def _attn_out_kernel(o_ref, x_ref, w_ref, g_ref, b_ref, y_ref):
    _post_norm_store(y_ref, x_ref[...], _dot(o_ref[...], w_ref[...]), g_ref, b_ref)


def _attn_out(o, x, w, g, b, *, tm):
    t, dm = x.shape
    row = pl.BlockSpec((tm, dm), lambda i: (i, 0))
    full = lambda a: pl.BlockSpec(a.shape, lambda i: (0, 0))
    return pl.pallas_call(
        _attn_out_kernel, grid=(t // tm,),
        in_specs=[row, row, full(w), full(g), full(b)],
        out_specs=row, out_shape=jax.ShapeDtypeStruct((t, dm), F32),
        compiler_params=_cparams("parallel"), name="attn_out",
    )(o, x, w, g, b)


def _dot_t(w_t, x):
    return lax.dot_general(w_t, x, (((1,), (1,)), ((), ())), preferred_element_type=F32)


def _qkv_prompt_kernel(x_ref, wqt_ref, wk_ref, wv_ref, wvt_ref, qt_ref, k_ref, v_ref, kb_ref, vt_ref):
    tm = x_ref.shape[0]
    n_kv = wk_ref.shape[1] // HEAD_DIM
    xb = x_ref[...].astype(BF16)
    qt_ref[...] = (_dot_t(wqt_ref[...], xb) * QUERY_SCALE).astype(qt_ref.dtype)
    k = _dot(xb, wk_ref[...])
    v = _dot(xb, wv_ref[...])
    kb_ref[...] = k.astype(kb_ref.dtype)
    vt_ref[0] = _dot_t(wvt_ref[...], xb).astype(vt_ref.dtype)
    for h in range(n_kv):
        k_ref[pl.ds(h, tm, stride=n_kv), :] = k[:, h * HEAD_DIM:(h + 1) * HEAD_DIM]
        v_ref[pl.ds(h, tm, stride=n_kv), :] = v[:, h * HEAD_DIM:(h + 1) * HEAD_DIM]


def _qkv_prompt(x, wqt, wk, wv, wvt, *, tm):
    t, dm = x.shape
    dq, dkv = wqt.shape[0], wk.shape[1]
    n_kv = dkv // HEAD_DIM
    row = lambda n: pl.BlockSpec((tm, n), lambda i: (i, 0))
    head_rows = pl.BlockSpec((tm * n_kv, HEAD_DIM), lambda i: (i, 0))
    const = lambda a: pl.BlockSpec(a.shape, lambda i: (0, 0), pipeline_mode=pl.Buffered(1))
    return pl.pallas_call(
        _qkv_prompt_kernel, grid=(t // tm,),
        in_specs=[row(dm), const(wqt), const(wk), const(wv), const(wvt)],
        out_specs=[pl.BlockSpec((dq, tm), lambda i: (0, i)), head_rows, head_rows, row(dkv),
                   pl.BlockSpec((1, dkv, tm), lambda i: (i, 0, 0))],
        out_shape=[jax.ShapeDtypeStruct((dq, t), BF16), jax.ShapeDtypeStruct((t * n_kv, HEAD_DIM), F32),
                   jax.ShapeDtypeStruct((t * n_kv, HEAD_DIM), F32), jax.ShapeDtypeStruct((t, dkv), BF16),
                   jax.ShapeDtypeStruct((t // tm, dkv, tm), BF16)],
        compiler_params=_cparams("parallel"), name="qkv_prompt",
    )(x, wqt, wk, wv, wvt)


def _qkv_sample_kernel(x_ref, wq_ref, wk_ref, wv_ref, q_ref, k_ref, v_ref):
    xb = x_ref[...].astype(BF16)
    q_ref[...] = (_dot(xb, wq_ref[...]) * QUERY_SCALE).astype(q_ref.dtype)
    k_ref[...] = _dot(xb, wk_ref[...])
    v_ref[...] = _dot(xb, wv_ref[...])


def _qkv_sample(x, wq, wk, wv, *, tm):
    t, dm = x.shape
    dq, dkv = wq.shape[1], wk.shape[1]
    row = lambda n: pl.BlockSpec((tm, n), lambda i: (i, 0))
    const = lambda a: pl.BlockSpec(a.shape, lambda i: (0, 0), pipeline_mode=pl.Buffered(1))
    return pl.pallas_call(
        _qkv_sample_kernel, grid=(t // tm,),
        in_specs=[row(dm), const(wq), const(wk), const(wv)],
        out_specs=[row(dq), row(dkv), row(dkv)],
        out_shape=[jax.ShapeDtypeStruct((t, dq), BF16), jax.ShapeDtypeStruct((t, dkv), F32),
                   jax.ShapeDtypeStruct((t, dkv), F32)],
        compiler_params=_cparams("parallel"), name="qkv_sample",
    )(x, wq, wk, wv)


MASKED_LOGIT = -1e30

EXP2_CLAMP = 126.0


def _softplus2(z):
    return jnp.maximum(jnp.log(1.0 + jnp.exp2(jnp.minimum(z, EXP2_CLAMP))) * LOG2E, z)


def _suffix_ones(n):
    return (lax.broadcasted_iota(jnp.int32, (n, n), 0) >= lax.broadcasted_iota(jnp.int32, (n, n), 1)).astype(BF16)


def _suffix_ones_t(n):
    return (lax.broadcasted_iota(jnp.int32, (n, n), 1) >= lax.broadcasted_iota(jnp.int32, (n, n), 0)).astype(BF16)


def _sb_scores(z, mask):
    if mask is not None:
        z = jnp.where(mask, z, MASKED_LOGIT)
    return z, _softplus2(z).astype(BF16)


def _sb_weights(z, c):
    return jnp.exp2(z - c).astype(BF16)


def _attn_prompt_kernel(bias_ref, qt_ref, k_ref, vt_ref, o_ref, acc_ref, carry_ref,
                        qa_ref, z0_ref, sp0_ref, z1_ref, sp1_ref, *, tq, q_per_kv, n_str):
    qi = pl.program_id(1)
    n = q_per_kv * tq
    u = _suffix_ones_t(tq)
    streams = range(n_str)
    head = lambda j: slice(j * HEAD_DIM, (j + 1) * HEAD_DIM)
    row = lax.broadcasted_iota(jnp.int32, (HEAD_DIM, n), 0)
    for j in streams:
        b2 = bias_ref[j] * LOG2E
        b_hi = b2.astype(BF16).astype(F32)
        extra = jnp.where(row == 0, b_hi, jnp.where(row == 1, b2 - b_hi, 0.0))
        qa_ref[j, :HEAD_DIM] = jnp.concatenate([qt_ref[head(j * q_per_kv + g), :] for g in range(q_per_kv)], axis=1)
        qa_ref[j, HEAD_DIM:] = extra.astype(BF16)
    ones2 = (lax.broadcasted_iota(jnp.int32, (tq, HEAD_DIM), 1) < 2).astype(BF16)
    acc_ref[...] = jnp.zeros_like(acc_ref)
    carry_ref[...] = jnp.zeros_like(carry_ref)
    stage = ((z0_ref, sp0_ref), (z1_ref, sp1_ref))

    def scores(kj, j):
        kb = k_ref[pl.ds(pl.multiple_of(kj * tq, tq), tq), head(j)]
        return _dot(jnp.concatenate([kb, ones2], axis=1), qa_ref[j])

    def suffix_sums(slot, j):
        return _dot(u, stage[slot][1][j]) + carry_ref[j]

    def weigh(kj, slot, j, c):
        acc_ref[j] += _dot(vt_ref[kj, head(j), :], _sb_weights(stage[slot][0][j], c))
        carry_ref[j] = c[:1, :]

    def score(kj, slot, mask):
        z_ref, sp_ref = stage[slot]
        for j in streams:
            z_ref[j], sp_ref[j] = _sb_scores(scores(kj, j), mask)

    def accumulate(kj, slot):
        cs = [suffix_sums(slot, j) for j in streams]
        for j in streams:
            weigh(kj, slot, j, cs[j])

    def step(kj, slot):
        zn_ref, spn_ref = stage[1 - slot]
        ss = [scores(kj - 1, j) for j in streams]
        cs = [suffix_sums(slot, j) for j in streams]
        for j in streams:
            zn_ref[j], spn_ref[j] = _sb_scores(ss[j], None)
        for j in streams:
            weigh(kj, slot, j, cs[j])

    key = lax.broadcasted_iota(jnp.int32, (tq, n), 0)
    query = lax.broadcasted_iota(jnp.int32, (tq, n), 1) % tq
    score(qi, 0, key < query)

    def pair(m, _):
        kj = qi - 2 * m
        step(kj, 0)
        step(kj - 1, 1)
        return 0

    lax.fori_loop(0, qi // 2, pair, 0)
    odd = qi % 2

    @pl.when(odd == 1)
    def _():
        step(1, 0)
        accumulate(0, 1)

    @pl.when(odd == 0)
    def _():
        accumulate(0, 0)

    o_ref[...] = jnp.concatenate([acc_ref[j, :, g * tq:(g + 1) * tq].T for j in streams for g in range(q_per_kv)],
                                 axis=1).astype(o_ref.dtype)


def _attn_prompt(bias_row, qt, kb, vt, *, tq, n_str):
    dq, t = qt.shape
    nb = t // tq
    n_kv = kb.shape[1] // HEAD_DIM
    q_per_kv = dq // HEAD_DIM // n_kv
    n = q_per_kv * tq
    assert vt.shape == (nb, n_kv * HEAD_DIM, tq) and bias_row.shape == (n_kv, 1, n) and n_kv % n_str == 0
    per_stream = lambda shape, dt: pltpu.VMEM((n_str,) + shape, dt)
    return pl.pallas_call(
        functools.partial(_attn_prompt_kernel, tq=tq, q_per_kv=q_per_kv, n_str=n_str),
        grid=(n_kv // n_str, nb),
        in_specs=[pl.BlockSpec((n_str, 1, n), lambda h, i: (h, 0, 0)),
                  pl.BlockSpec((n_str * q_per_kv * HEAD_DIM, tq), lambda h, i: (h, i)),
                  pl.BlockSpec((t, n_str * HEAD_DIM), lambda h, i: (0, h)),
                  pl.BlockSpec((nb, n_str * HEAD_DIM, tq), lambda h, i: (0, h, 0))],
        out_specs=pl.BlockSpec((tq, n_str * q_per_kv * HEAD_DIM), lambda h, i: (i, h)),
        out_shape=jax.ShapeDtypeStruct((t, dq), BF16),
        scratch_shapes=[per_stream((HEAD_DIM, n), F32), per_stream((1, n), F32), per_stream((2 * HEAD_DIM, n), BF16)]
        + [per_stream((tq, n), F32), per_stream((tq, n), BF16)] * 2,
        compiler_params=_cparams("parallel", "arbitrary"), name="attn_prompt",
    )(bias_row, qt, kb, vt)


def _attn_sample_kernel(pt_ref, q_ref, bias_ref, kn_ref, vn_ref, *refs, n_step, n_pages, n_sub):
    n_refs = n_sub * n_pages
    kc_refs, vc_refs, o_ref = refs[:n_refs], refs[n_refs:2 * n_refs], refs[2 * n_refs]
    rows = q_ref.shape[1]
    n_kv = kn_ref.shape[2] // HEAD_DIM
    page = kc_refs[0].shape[1] // n_kv
    rows_per_kv = rows // n_kv
    row_kv = lax.broadcasted_iota(jnp.int32, (rows, HEAD_DIM), 0) // rows_per_kv
    bias = bias_ref[...] * LOG2E

    def page_2d(ref):
        heads = [ref[0, pl.ds(h, page, stride=n_kv), :] for h in range(n_kv)]
        return jnp.concatenate(heads, axis=1).astype(BF16)

    def new_tokens(ref, s):
        pad = jnp.zeros((page - n_step, n_kv * HEAD_DIM), F32)
        return jnp.concatenate([ref[s], pad], axis=0).astype(BF16)

    def page_pair(page_refs, s, m):
        return jnp.concatenate([page_2d(page_refs[s * n_pages + 2 * m]),
                                page_2d(page_refs[s * n_pages + 2 * m + 1])], axis=0)

    def score(q_bd, k2, mask):
        s = lax.dot_general(q_bd, k2, (((1,), (1,)), ((), ())), preferred_element_type=F32)
        return _sb_scores(s + bias, mask)

    step = lax.broadcasted_iota(jnp.int32, (rows, page), 0) % n_step
    key = lax.broadcasted_iota(jnp.int32, (rows, page), 1)
    pairs = list(reversed(range(n_pages // 2)))
    u1, u2 = _suffix_ones(page), _suffix_ones(2 * page)
    scored = []
    for s in range(n_sub):
        q = q_ref[s]
        q_bd = jnp.concatenate([jnp.where(row_kv == h, q, jnp.zeros_like(q)) for h in range(n_kv)], axis=1)
        scored.append([score(q_bd, new_tokens(kn_ref, s), key < step)]
                      + [score(q_bd, page_pair(kc_refs, s, m), None) for m in pairs])
    sums = [[_dot(sp, u1 if i == 0 else u2) for i, (_, sp) in enumerate(blocks)] for blocks in scored]
    for s in range(n_sub):
        values = [new_tokens(vn_ref, s)] + [page_pair(vc_refs, s, m) for m in pairs]
        acc = jnp.zeros((rows, n_kv * HEAD_DIM), F32)
        carry = jnp.zeros((rows, 1), F32)
        for (z, _), c_local, v2 in zip(scored[s], sums[s], values):
            c = c_local + carry
            acc = acc + _dot(_sb_weights(z, c), v2)
            carry = c[:, :1]
        o = jnp.zeros((rows, HEAD_DIM), F32)
        for h in range(n_kv):
            o = o + jnp.where(row_kv == h, acc[:, h * HEAD_DIM:(h + 1) * HEAD_DIM], 0.0)
        o_ref[s] = o.astype(o_ref.dtype)


def _page_index(seq, n_sub, p, b, pt):
    return (pt[b * n_sub + seq, p], 0, 0)


def _attn_sample(page_table, q, bias_col, cache_k, cache_v, k_new, v_new, *, n_sub):
    n_seq, rows, _ = q.shape
    n_pages = page_table.shape[1]
    _, page, n_kv, _ = cache_k.shape
    n_step = k_new.shape[1]
    assert page % SUBLANES == 0 and n_step <= page and n_pages % 2 == 0 and n_seq % n_sub == 0
    per_seq = lambda shape: pl.BlockSpec((n_sub,) + shape, lambda b, pt: (b,) + (0,) * len(shape))
    cache_k = cache_k.reshape(-1, page * n_kv, HEAD_DIM)
    cache_v = cache_v.reshape(-1, page * n_kv, HEAD_DIM)
    pages = [pl.BlockSpec((1, page * n_kv, HEAD_DIM), functools.partial(_page_index, s, n_sub, p))
             for s in range(n_sub) for p in range(n_pages)]
    grid_spec = pltpu.PrefetchScalarGridSpec(
        num_scalar_prefetch=1, grid=(n_seq // n_sub,),
        in_specs=[per_seq((rows, HEAD_DIM)), pl.BlockSpec((rows, 1), lambda b, pt: (0, 0)),
                  per_seq((n_step, n_kv * HEAD_DIM)), per_seq((n_step, n_kv * HEAD_DIM))] + pages + pages,
        out_specs=per_seq((rows, HEAD_DIM)))
    return pl.pallas_call(
        functools.partial(_attn_sample_kernel, n_step=n_step, n_pages=n_pages, n_sub=n_sub),
        grid_spec=grid_spec, out_shape=jax.ShapeDtypeStruct((n_seq, rows, HEAD_DIM), BF16),
        compiler_params=_cparams("parallel"), name="attn_sample",
    )(page_table, q, bias_col, k_new, v_new, *([cache_k] * len(pages)), *([cache_v] * len(pages)))


def kernel(x_prompt, x_sample, p_prompt, p_sample, state_ssm_re, state_ssm_im, cache_k, cache_v, page_table, s5_lam_re, s5_lam_im, s5_log_dt, s5_b_re, s5_b_im, s5_c_re, s5_c_im, s5_d, s5_w_gate, s5_w_out, w_k, w_v, sb_w_q, sb_w_o, sb_bias, ln_mix_g, ln_mix_b, mlp_w1, mlp_w2, ln_mlp_g, ln_mlp_b, ple_w_proj, ple_w_gate, ln_ple_g, ln_ple_b):
    _, t_p, dm = x_prompt.shape
    n_seq, n_step, _ = x_sample.shape
    n_grp_total, p_state = s5_lam_re.shape[1:]
    n_blk = dm // CH_BLOCK
    gpb = CH_BLOCK // SSM_GROUP
    ns = gpb * p_state
    n_heads = sb_w_q.shape[2] // HEAD_DIM
    rows_s = n_seq * n_step
    assert x_prompt.shape[0] == 1 and n_step == SUBLANES and n_seq % SUBLANES == 0
    bf = lambda a: a.astype(BF16)
    vec = lambda a: a.reshape(1, -1)

    def to_group_order(a):
        c = a.shape[-1]
        return a.reshape(n_seq // SUBLANES, SUBLANES, n_step, c).transpose(0, 2, 1, 3).reshape(rows_s, c)

    def to_batch_order(a):
        c = a.shape[-1]
        return a.reshape(n_seq // SUBLANES, n_step, SUBLANES, c).transpose(0, 2, 1, 3).reshape(n_seq, n_step, c)

    xp = x_prompt.reshape(t_p, dm)
    xs = to_group_order(x_sample)
    pp = [p_prompt[i].reshape(t_p, -1) for i in range(DEPTH)]
    ps = [to_group_order(p_sample[i]) for i in range(DEPTH)]

    w_gate, w_out, w_o = bf(s5_w_gate[0]), bf(s5_w_out[0]), bf(sb_w_o[0])
    w1, w2, wp, wg = bf(mlp_w1), bf(mlp_w2), bf(ple_w_proj), bf(ple_w_gate)
    wq, wk, wv = bf(sb_w_q[0]), bf(w_k), bf(w_v)

    def glu(g, x):
        return _glu(g, x, w_gate, w_out, vec(ln_mix_g[0]), vec(ln_mix_b[0]), tm=512, tf=1024)

    def attn_out(o, x):
        return _attn_out(o, x, w_o, vec(ln_mix_g[1]), vec(ln_mix_b[1]), tm=512)

    def mlp(x, i):
        return _mlp(x, w1, w2, i, vec(ln_mlp_g[i]), vec(ln_mlp_b[i]), tm=512, tf=1024)

    def ple(x, p, i):
        return _ple(x, p, wp, wg, i, vec(ln_ple_g[i]), vec(ln_ple_b[i]), tm=512)

    ab_re, ab_im, bb_re, bb_im = _s5_prep(s5_lam_re[0], s5_lam_im[0], s5_log_dt[0], s5_b_re[0], s5_b_im[0])
    wb = bf(jnp.concatenate([_block_diag_in(bb_re, n_blk, gpb, p_state),
                             _block_diag_in(bb_im, n_blk, gpb, p_state)], axis=-1))
    wc_re = bf(_block_diag_out(s5_c_re[0], n_blk, gpb, p_state))
    wc_im = bf(_block_diag_out(s5_c_im[0], n_blk, gpb, p_state))
    a_re = ab_re.reshape(n_blk, ns)
    a_im = ab_im.reshape(n_blk, ns)
    d = vec(s5_d[0])
    g_p, hp_re, hp_im = _scan_prompt(xp, wb, wc_re, wc_im, a_re, a_im, d, tb=128)
    g_s, hs_re, hs_im = _scan_sample(xs, wb, wc_re, wc_im, a_re, a_im, d,
                                     state_ssm_re[0].reshape(n_seq, -1), state_ssm_im[0].reshape(n_seq, -1),
                                     n_step=n_step)
    xp = ple(mlp(glu(g_p, xp), 0), pp[0], 0)
    xs = ple(mlp(glu(g_s, xs), 0), ps[0], 0)

    tq = 256
    q_per_kv = n_heads // N_KV_HEADS
    qt, k_p, v_p, kb, vt = _qkv_prompt(xp, wq.T, wk, wv, wv.T, tm=tq)
    bias_row = jnp.repeat(sb_bias[0].reshape(N_KV_HEADS, q_per_kv), tq, axis=1)
    o_p = _attn_prompt(bias_row.reshape(N_KV_HEADS, 1, q_per_kv * tq), qt, kb, vt, tq=tq, n_str=2)
    q_s, k_s, v_s = _qkv_sample(xs, wq, wk, wv, tm=256)
    k_s = to_batch_order(k_s)
    v_s = to_batch_order(v_s)
    q_s = to_batch_order(q_s).reshape(n_seq, n_step, n_heads, HEAD_DIM).transpose(0, 2, 1, 3)
    bias_col = jnp.repeat(sb_bias[0], n_step).reshape(n_heads * n_step, 1)
    o_s = _attn_sample(page_table, q_s.reshape(n_seq, n_heads * n_step, HEAD_DIM), bias_col,
                       cache_k, cache_v, k_s, v_s, n_sub=1)
    o_s = o_s.reshape(n_seq, n_heads, n_step, HEAD_DIM).transpose(0, 2, 1, 3).reshape(n_seq, n_step, -1)
    xp = ple(mlp(attn_out(o_p, xp), 1), pp[1], 1)
    xs = ple(mlp(attn_out(to_group_order(o_s), xs), 1), ps[1], 1)

    state = lambda a: a.reshape(1, -1, n_grp_total, p_state)
    kv = lambda a, lead: a.reshape(lead + (N_KV_HEADS, HEAD_DIM))
    return (xp.reshape(1, t_p, dm), to_batch_order(xs),
            state(hp_re), state(hp_im), kv(k_p, (1, t_p)), kv(v_p, (1, t_p)),
            state(hs_re), state(hs_im), kv(k_s, (n_seq, n_step)), kv(v_s, (n_seq, n_step)))
```

```python
import functools
import math

import jax
import jax.numpy as jnp
from jax import lax
from jax.experimental import pallas as pl
from jax.experimental.pallas import tpu as pltpu

F32 = jnp.float32
BF16 = jnp.bfloat16

SSM_GROUP = 16
HEAD_DIM = 128
N_KV_HEADS = 4
LN_EPS = 1e-5
DEPTH = 2
DN_ALPHA = (2.0 * DEPTH) ** 0.25
LOG2E = 1.4426950408889634
QUERY_SCALE = HEAD_DIM ** -0.5 * LOG2E

SUBLANES = 8
LANES = 128
CH_BLOCK = 256
STEP_PITCH = 17
BLOCK_PITCH = 2
VMEM_LIMIT = 56 * 1024 * 1024
NORM_ROWS = 256


def _cparams(*sem):
    return pltpu.CompilerParams(dimension_semantics=sem, vmem_limit_bytes=VMEM_LIMIT)


def _dot(a, b):
    return jnp.dot(a, b, preferred_element_type=F32)


def _sigmoid(x):
    return 1.0 / (1.0 + jnp.exp(-x))


def _layer_norm_rows(y, g, b):
    mu = jnp.mean(y, axis=-1, keepdims=True)
    yc = y - mu
    var = jnp.mean(yc * yc, axis=-1, keepdims=True)
    return yc * lax.rsqrt(var + LN_EPS) * g + b


def _discretize(lam_re, lam_im, dt):
    mag = jnp.exp(lam_re * dt)
    ang = lam_im * dt
    ab_re = mag * jnp.cos(ang)
    ab_im = mag * jnp.sin(ang)
    den = lam_re * lam_re + lam_im * lam_im
    nr = ab_re - 1.0
    f_re = (nr * lam_re + ab_im * lam_im) / den
    f_im = (ab_im * lam_re - nr * lam_im) / den
    return ab_re, ab_im, f_re, f_im


def _s5_prep_kernel(lre_ref, lim_ref, ldt_ref, lre_rep_ref, lim_rep_ref, bre_ref, bim_ref,
                    abre_ref, abim_ref, bbre_ref, bbim_ref):
    dt = jnp.exp(ldt_ref[...])
    ab_re, ab_im, _, _ = _discretize(lre_ref[...], lim_ref[...], dt)
    abre_ref[...] = ab_re
    abim_ref[...] = ab_im
    _, _, f_re, f_im = _discretize(lre_rep_ref[...], lim_rep_ref[...], dt)
    b_re = bre_ref[...]
    b_im = bim_ref[...]
    bbre_ref[...] = f_re * b_re - f_im * b_im
    bbim_ref[...] = f_re * b_im + f_im * b_re


def _s5_prep(lam_re, lam_im, log_dt, b_re, b_im):
    g, p = lam_re.shape
    rep = lambda a: jnp.repeat(a, SSM_GROUP, axis=1)
    out = [jax.ShapeDtypeStruct((g, p), F32)] * 2 + [jax.ShapeDtypeStruct((g, p * SSM_GROUP), F32)] * 2
    return pl.pallas_call(_s5_prep_kernel, out_shape=out, name="s5_prep")(
        lam_re, lam_im, log_dt.reshape(g, 1), rep(lam_re), rep(lam_im),
        b_re.reshape(g, p * SSM_GROUP), b_im.reshape(g, p * SSM_GROUP))


def _block_diag(v, width):
    n_blk, rows, _ = v.shape
    gpb = CH_BLOCK // SSM_GROUP
    tiled = jnp.tile(v, (1, 1, gpb))
    row_grp = lax.broadcasted_iota(jnp.int32, (rows, gpb * width), 0) // (rows // gpb)
    col_grp = lax.broadcasted_iota(jnp.int32, (rows, gpb * width), 1) // width
    return jnp.where((row_grp == col_grp)[None], tiled, 0.0).astype(BF16)


def _block_diag_in(bb, n_blk, gpb, p):
    v = bb.reshape(n_blk, gpb, p, SSM_GROUP).transpose(0, 1, 3, 2)
    return _block_diag(v.reshape(n_blk, gpb * SSM_GROUP, p), p)


def _block_diag_out(c, n_blk, gpb, p):
    v = c.reshape(n_blk, gpb, SSM_GROUP, p).transpose(0, 1, 3, 2)
    return _block_diag(v.reshape(n_blk, gpb * p, SSM_GROUP), SSM_GROUP)


def _complex_step(a_re, a_im, h_re, h_im, b_re, b_im):
    return (a_re * h_re - a_im * h_im + b_re, a_re * h_im + a_im * h_re + b_im)


def _scan_prompt_kernel(x_ref, wb_ref, wcre_ref, wcim_ref, are_ref, aim_ref, d_ref,
                        g_ref, hre_ref, him_ref, s_ref, h_ref, *, n_blk, tb, ns):
    nc = ns // LANES
    block_rows = lambda j: pl.ds(j * BLOCK_PITCH, tb, stride=STEP_PITCH)

    @pl.when(pl.program_id(0) == 0)
    def _():
        h_ref[...] = jnp.zeros_like(h_ref)

    x = x_ref[...]
    for j in range(n_blk):
        u = x[:, j * CH_BLOCK:(j + 1) * CH_BLOCK].astype(BF16)
        bu = _dot(u, wb_ref[j])
        for c in range(2 * nc):
            s_ref[c, block_rows(j), :] = bu[:, c * LANES:(c + 1) * LANES]
    a_re = are_ref[...]
    a_im = aim_ref[...]

    def body(t, carry):
        rows = pl.ds(t * STEP_PITCH, n_blk, stride=BLOCK_PITCH)
        b_re = jnp.concatenate([s_ref[c, rows, :] for c in range(nc)], axis=1)
        b_im = jnp.concatenate([s_ref[nc + c, rows, :] for c in range(nc)], axis=1)
        h_re, h_im = _complex_step(a_re, a_im, carry[0], carry[1], b_re, b_im)
        for c in range(nc):
            s_ref[c, rows, :] = h_re[:, c * LANES:(c + 1) * LANES]
            s_ref[nc + c, rows, :] = h_im[:, c * LANES:(c + 1) * LANES]
        return h_re, h_im

    h_re, h_im = lax.fori_loop(0, tb, body, (h_ref[:, :ns], h_ref[:, ns:]), unroll=8)
    h_ref[...] = jnp.concatenate([h_re, h_im], axis=1)
    hre_ref[...] = h_re
    him_ref[...] = h_im

    ys = []
    for j in range(n_blk):
        hj_re = jnp.concatenate([s_ref[c, block_rows(j), :] for c in range(nc)], axis=1)
        hj_im = jnp.concatenate([s_ref[nc + c, block_rows(j), :] for c in range(nc)], axis=1)
        ys.append(_dot(hj_re.astype(BF16), wcre_ref[j]) - _dot(hj_im.astype(BF16), wcim_ref[j]))
    y = jnp.concatenate(ys, axis=1) + d_ref[...] * x
    g_ref[...] = jax.nn.gelu(y).astype(g_ref.dtype)


def _scan_prompt(x, wb, wc_re, wc_im, a_re, a_im, d, *, tb):
    t, dm = x.shape
    n_blk, _, ns2 = wb.shape
    ns = ns2 // 2
    assert n_blk == SUBLANES and t % tb == 0
    assert (n_blk - 1) * BLOCK_PITCH < STEP_PITCH
    full = lambda shape: pl.BlockSpec(shape, lambda i: (0,) * len(shape))
    const = lambda shape: pl.BlockSpec(shape, lambda i: (0,) * len(shape), pipeline_mode=pl.Buffered(1))
    return pl.pallas_call(
        functools.partial(_scan_prompt_kernel, n_blk=n_blk, tb=tb, ns=ns),
        grid=(t // tb,),
        in_specs=[pl.BlockSpec((tb, dm), lambda i: (i, 0)), const(wb.shape), const(wc_re.shape),
                  const(wc_im.shape), full(a_re.shape), full(a_im.shape), full(d.shape)],
        out_specs=[pl.BlockSpec((tb, dm), lambda i: (i, 0)), full((n_blk, ns)), full((n_blk, ns))],
        out_shape=[jax.ShapeDtypeStruct((t, dm), BF16), jax.ShapeDtypeStruct((n_blk, ns), F32),
                   jax.ShapeDtypeStruct((n_blk, ns), F32)],
        scratch_shapes=[pltpu.VMEM((ns2 // LANES, tb * STEP_PITCH, LANES), F32), pltpu.VMEM((n_blk, ns2), F32)],
        compiler_params=_cparams("arbitrary"),
        name="s5_scan_prompt",
    )(x, wb, wc_re, wc_im, a_re, a_im, d)


def _scan_sample_kernel(x_ref, wb_ref, wcre_ref, wcim_ref, are_ref, aim_ref, d_ref, h0re_ref, h0im_ref,
                        g_ref, hre_ref, him_ref, s_ref, *, n_grp, n_step, ns):
    x = x_ref[...]
    s_ref[...] = _dot(x.astype(BF16), wb_ref[0])
    a_re = jnp.broadcast_to(are_ref[0], (SUBLANES, ns))
    a_im = jnp.broadcast_to(aim_ref[0], (SUBLANES, ns))

    def body(bg, _):
        b0 = pl.multiple_of(bg * SUBLANES, SUBLANES)
        h_re = h0re_ref[pl.ds(b0, SUBLANES), :]
        h_im = h0im_ref[pl.ds(b0, SUBLANES), :]
        for t in range(n_step):
            r0 = pl.multiple_of((bg * n_step + t) * SUBLANES, SUBLANES)
            tile = s_ref[pl.ds(r0, SUBLANES), :]
            h_re, h_im = _complex_step(a_re, a_im, h_re, h_im, tile[:, :ns], tile[:, ns:])
            s_ref[pl.ds(r0, SUBLANES), :] = jnp.concatenate([h_re, h_im], axis=1)
        hre_ref[pl.ds(b0, SUBLANES), :] = h_re
        him_ref[pl.ds(b0, SUBLANES), :] = h_im
        return 0

    lax.fori_loop(0, n_grp, body, 0)
    h = s_ref[...]
    y = _dot(h[:, :ns].astype(BF16), wcre_ref[0]) - _dot(h[:, ns:].astype(BF16), wcim_ref[0])
    g_ref[...] = jax.nn.gelu(y + d_ref[...] * x).astype(g_ref.dtype)


def _scan_sample(x, wb, wc_re, wc_im, a_re, a_im, d, h0_re, h0_im, *, n_step):
    rows, dm = x.shape
    n_blk, _, ns2 = wb.shape
    ns = ns2 // 2
    n_seq = h0_re.shape[0]
    assert rows == n_seq * n_step and n_seq % SUBLANES == 0
    col = lambda shape: pl.BlockSpec(shape, lambda j: (0, j))
    lead = lambda shape: pl.BlockSpec(shape, lambda j: (j,) + (0,) * (len(shape) - 1))
    return pl.pallas_call(
        functools.partial(_scan_sample_kernel, n_grp=n_seq // SUBLANES, n_step=n_step, ns=ns),
        grid=(n_blk,),
        in_specs=[col((rows, CH_BLOCK)), lead((1, CH_BLOCK, ns2)), lead((1, ns, CH_BLOCK)),
                  lead((1, ns, CH_BLOCK)), lead((1, 1, ns)), lead((1, 1, ns)), col((1, CH_BLOCK)),
                  col((n_seq, ns)), col((n_seq, ns))],
        out_specs=[col((rows, CH_BLOCK)), col((n_seq, ns)), col((n_seq, ns))],
        out_shape=[jax.ShapeDtypeStruct((rows, dm), BF16), jax.ShapeDtypeStruct((n_seq, n_blk * ns), F32),
                   jax.ShapeDtypeStruct((n_seq, n_blk * ns), F32)],
        scratch_shapes=[pltpu.VMEM((rows, ns2), F32)],
        compiler_params=_cparams("arbitrary"),
        name="s5_scan_sample",
    )(x, wb, wc_re, wc_im, a_re.reshape(n_blk, 1, ns), a_im.reshape(n_blk, 1, ns), d, h0_re, h0_im)


def _post_norm_store(o_ref, x, sub, g_ref, b_ref):
    o_ref[...] = _layer_norm_rows(DN_ALPHA * x + sub, g_ref[...], b_ref[...])


def _glu_kernel(a_ref, x_ref, w1_ref, w2_ref, g_ref, b_ref, o_ref):
    a = a_ref[...]
    h = a.astype(F32) * _sigmoid(_dot(a, w1_ref[...]))
    _post_norm_store(o_ref, x_ref[...], _dot(h.astype(BF16), w2_ref[...]), g_ref, b_ref)


def _glu(a, x, w1, w2, g, b, *, tm):
    t, dm = x.shape
    row = pl.BlockSpec((tm, dm), lambda i: (i, 0))
    const = lambda arr: pl.BlockSpec(arr.shape, lambda i: (0, 0), pipeline_mode=pl.Buffered(1))
    return pl.pallas_call(
        _glu_kernel, grid=(t // tm,),
        in_specs=[row, row, const(w1), const(w2), const(g), const(b)],
        out_specs=row, out_shape=jax.ShapeDtypeStruct((t, dm), F32),
        compiler_params=_cparams("parallel"), name="glu",
    )(a, x, w1, w2, g, b)


def _mlp_kernel(x_ref, w1_ref, w2_ref, g_ref, b_ref, o_ref, xb_ref):
    f = pl.program_id(1)

    @pl.when(f == 0)
    def _():
        xb_ref[...] = x_ref[...].astype(BF16)
        o_ref[...] = jnp.zeros_like(o_ref)

    h = jnp.square(jnp.maximum(_dot(xb_ref[...], w1_ref[...].astype(BF16)), 0.0))
    o_ref[...] += _dot(h.astype(BF16), w2_ref[...].astype(BF16))

    @pl.when(f == pl.num_programs(1) - 1)
    def _():
        for r0 in range(0, o_ref.shape[0], NORM_ROWS):
            rows = slice(r0, r0 + NORM_ROWS)
            o_ref[rows] = _layer_norm_rows(DN_ALPHA * x_ref[rows] + o_ref[rows], g_ref[...], b_ref[...])


def _mlp(x, w1, w2, layer, g, b, *, tm, tf):
    t, dm = x.shape
    once = pl.Buffered(1)
    row = pl.BlockSpec((tm, dm), lambda i, f: (i, 0), pipeline_mode=once)
    vec = pl.BlockSpec((1, dm), lambda i, f: (0, 0), pipeline_mode=once)
    w1s = pl.BlockSpec((None, dm, tf), lambda i, f: (layer, 0, f))
    w2s = pl.BlockSpec((None, tf, dm), lambda i, f: (layer, f, 0))
    return pl.pallas_call(
        _mlp_kernel, grid=(t // tm, w1.shape[2] // tf),
        in_specs=[row, w1s, w2s, vec, vec],
        out_specs=pl.BlockSpec((tm, dm), lambda i, f: (i, 0)),
        out_shape=jax.ShapeDtypeStruct((t, dm), F32),
        scratch_shapes=[pltpu.VMEM((tm, dm), BF16)],
        compiler_params=_cparams("parallel", "arbitrary"), name="mlp",
    )(x, w1, w2, g, b)


def _ple_kernel(x_ref, p_ref, wp_ref, wg_ref, g_ref, b_ref, o_ref):
    x = x_ref[...]
    sub = _dot(p_ref[...].astype(BF16), wp_ref[...]) * _sigmoid(_dot(x.astype(BF16), wg_ref[...]))
    _post_norm_store(o_ref, x, sub, g_ref, b_ref)


def _ple(x, p, wp, wg, layer, g, b, *, tm):
    t, dm = x.shape
    row = pl.BlockSpec((tm, dm), lambda i: (i, 0))
    full = lambda a: pl.BlockSpec(a.shape, lambda i: (0, 0))
    of_layer = lambda a: pl.BlockSpec((None,) + a.shape[1:], lambda i: (layer, 0, 0),
                                      pipeline_mode=pl.Buffered(1))
    return pl.pallas_call(
        _ple_kernel, grid=(t // tm,),
        in_specs=[row, pl.BlockSpec((tm, p.shape[1]), lambda i: (i, 0)), of_layer(wp), of_layer(wg),
                  full(g), full(b)],
        out_specs=row, out_shape=jax.ShapeDtypeStruct((t, dm), F32),
        compiler_params=_cparams("parallel"), name="ple",
    )(x, p, wp, wg, g, b)


def _attn_out_kernel(o_ref, x_ref, w_ref, g_ref, b_ref, y_ref):
    _post_norm_store(y_ref, x_ref[...], _dot(o_ref[...], w_ref[...]), g_ref, b_ref)


def _attn_out(o, x, w, g, b, *, tm):
    t, dm = x.shape
    row = pl.BlockSpec((tm, dm), lambda i: (i, 0))
    full = lambda a: pl.BlockSpec(a.shape, lambda i: (0, 0))
    return pl.pallas_call(
        _attn_out_kernel, grid=(t // tm,),
        in_specs=[row, row, full(w), full(g), full(b)],
        out_specs=row, out_shape=jax.ShapeDtypeStruct((t, dm), F32),
        compiler_params=_cparams("parallel"), name="attn_out",
    )(o, x, w, g, b)


def _dot_t(w_t, x):
    return lax.dot_general(w_t, x, (((1,), (1,)), ((), ())), preferred_element_type=F32)


def _qkv_prompt_kernel(x_ref, wqt_ref, wk_ref, wv_ref, wvt_ref, qt_ref, k_ref, v_ref, kb_ref, vt_ref):
    tm = x_ref.shape[0]
    n_kv = wk_ref.shape[1] // HEAD_DIM
    xb = x_ref[...].astype(BF16)
    qt_ref[...] = (_dot_t(wqt_ref[...], xb) * QUERY_SCALE).astype(qt_ref.dtype)
    k = _dot(xb, wk_ref[...])
    v = _dot(xb, wv_ref[...])
    kb_ref[...] = k.astype(kb_ref.dtype)
    vt_ref[0] = _dot_t(wvt_ref[...], xb).astype(vt_ref.dtype)
    for h in range(n_kv):
        k_ref[pl.ds(h, tm, stride=n_kv), :] = k[:, h * HEAD_DIM:(h + 1) * HEAD_DIM]
        v_ref[pl.ds(h, tm, stride=n_kv), :] = v[:, h * HEAD_DIM:(h + 1) * HEAD_DIM]


def _qkv_prompt(x, wqt, wk, wv, wvt, *, tm):
    t, dm = x.shape
    dq, dkv = wqt.shape[0], wk.shape[1]
    n_kv = dkv // HEAD_DIM
    row = lambda n: pl.BlockSpec((tm, n), lambda i: (i, 0))
    head_rows = pl.BlockSpec((tm * n_kv, HEAD_DIM), lambda i: (i, 0))
    const = lambda a: pl.BlockSpec(a.shape, lambda i: (0, 0), pipeline_mode=pl.Buffered(1))
    return pl.pallas_call(
        _qkv_prompt_kernel, grid=(t // tm,),
        in_specs=[row(dm), const(wqt), const(wk), const(wv), const(wvt)],
        out_specs=[pl.BlockSpec((dq, tm), lambda i: (0, i)), head_rows, head_rows, row(dkv),
                   pl.BlockSpec((1, dkv, tm), lambda i: (i, 0, 0))],
        out_shape=[jax.ShapeDtypeStruct((dq, t), BF16), jax.ShapeDtypeStruct((t * n_kv, HEAD_DIM), F32),
                   jax.ShapeDtypeStruct((t * n_kv, HEAD_DIM), F32), jax.ShapeDtypeStruct((t, dkv), BF16),
                   jax.ShapeDtypeStruct((t // tm, dkv, tm), BF16)],
        compiler_params=_cparams("parallel"), name="qkv_prompt",
    )(x, wqt, wk, wv, wvt)


def _qkv_sample_kernel(x_ref, wq_ref, wk_ref, wv_ref, q_ref, k_ref, v_ref):
    xb = x_ref[...].astype(BF16)
    q_ref[...] = (_dot(xb, wq_ref[...]) * QUERY_SCALE).astype(q_ref.dtype)
    k_ref[...] = _dot(xb, wk_ref[...])
    v_ref[...] = _dot(xb, wv_ref[...])


def _qkv_sample(x, wq, wk, wv, *, tm):
    t, dm = x.shape
    dq, dkv = wq.shape[1], wk.shape[1]
    row = lambda n: pl.BlockSpec((tm, n), lambda i: (i, 0))
    const = lambda a: pl.BlockSpec(a.shape, lambda i: (0, 0), pipeline_mode=pl.Buffered(1))
    return pl.pallas_call(
        _qkv_sample_kernel, grid=(t // tm,),
        in_specs=[row(dm), const(wq), const(wk), const(wv)],
        out_specs=[row(dq), row(dkv), row(dkv)],
        out_shape=[jax.ShapeDtypeStruct((t, dq), BF16), jax.ShapeDtypeStruct((t, dkv), F32),
                   jax.ShapeDtypeStruct((t, dkv), F32)],
        compiler_params=_cparams("parallel"), name="qkv_sample",
    )(x, wq, wk, wv)


MASKED_LOGIT = -1e30

EXP2_CLAMP = 126.0


def _softplus2(z):
    return jnp.maximum(jnp.log(1.0 + jnp.exp2(jnp.minimum(z, EXP2_CLAMP))) * LOG2E, z)


def _suffix_ones(n):
    return (lax.broadcasted_iota(jnp.int32, (n, n), 0) >= lax.broadcasted_iota(jnp.int32, (n, n), 1)).astype(BF16)


def _suffix_ones_t(n):
    return (lax.broadcasted_iota(jnp.int32, (n, n), 1) >= lax.broadcasted_iota(jnp.int32, (n, n), 0)).astype(BF16)


def _sb_scores(z, mask):
    if mask is not None:
        z = jnp.where(mask, z, MASKED_LOGIT)
    return z, _softplus2(z).astype(BF16)


def _sb_weights(z, c):
    return jnp.exp2(z - c).astype(BF16)


def _attn_prompt_kernel(bias_ref, qt_ref, k_ref, vt_ref, o_ref, acc_ref, carry_ref,
                        qa_ref, z0_ref, sp0_ref, z1_ref, sp1_ref, *, tq, q_per_kv, n_str):
    qi = pl.program_id(1)
    n = q_per_kv * tq
    u = _suffix_ones_t(tq)
    streams = range(n_str)
    head = lambda j: slice(j * HEAD_DIM, (j + 1) * HEAD_DIM)
    row = lax.broadcasted_iota(jnp.int32, (HEAD_DIM, n), 0)
    for j in streams:
        b2 = bias_ref[j] * LOG2E
        b_hi = b2.astype(BF16).astype(F32)
        extra = jnp.where(row == 0, b_hi, jnp.where(row == 1, b2 - b_hi, 0.0))
        qa_ref[j, :HEAD_DIM] = jnp.concatenate([qt_ref[head(j * q_per_kv + g), :] for g in range(q_per_kv)], axis=1)
        qa_ref[j, HEAD_DIM:] = extra.astype(BF16)
    ones2 = (lax.broadcasted_iota(jnp.int32, (tq, HEAD_DIM), 1) < 2).astype(BF16)
    acc_ref[...] = jnp.zeros_like(acc_ref)
    carry_ref[...] = jnp.zeros_like(carry_ref)
    stage = ((z0_ref, sp0_ref), (z1_ref, sp1_ref))

    def scores(kj, j):
        kb = k_ref[pl.ds(pl.multiple_of(kj * tq, tq), tq), head(j)]
        return _dot(jnp.concatenate([kb, ones2], axis=1), qa_ref[j])

    def suffix_sums(slot, j):
        return _dot(u, stage[slot][1][j]) + carry_ref[j]

    def weigh(kj, slot, j, c):
        acc_ref[j] += _dot(vt_ref[kj, head(j), :], _sb_weights(stage[slot][0][j], c))
        carry_ref[j] = c[:1, :]

    def score(kj, slot, mask):
        z_ref, sp_ref = stage[slot]
        for j in streams:
            z_ref[j], sp_ref[j] = _sb_scores(scores(kj, j), mask)

    def accumulate(kj, slot):
        cs = [suffix_sums(slot, j) for j in streams]
        for j in streams:
            weigh(kj, slot, j, cs[j])

    def step(kj, slot):
        zn_ref, spn_ref = stage[1 - slot]
        ss = [scores(kj - 1, j) for j in streams]
        cs = [suffix_sums(slot, j) for j in streams]
        for j in streams:
            zn_ref[j], spn_ref[j] = _sb_scores(ss[j], None)
        for j in streams:
            weigh(kj, slot, j, cs[j])

    key = lax.broadcasted_iota(jnp.int32, (tq, n), 0)
    query = lax.broadcasted_iota(jnp.int32, (tq, n), 1) % tq
    score(qi, 0, key < query)

    def pair(m, _):
        kj = qi - 2 * m
        step(kj, 0)
        step(kj - 1, 1)
        return 0

    lax.fori_loop(0, qi // 2, pair, 0)
    odd = qi % 2

    @pl.when(odd == 1)
    def _():
        step(1, 0)
        accumulate(0, 1)

    @pl.when(odd == 0)
    def _():
        accumulate(0, 0)

    o_ref[...] = jnp.concatenate([acc_ref[j, :, g * tq:(g + 1) * tq].T for j in streams for g in range(q_per_kv)],
                                 axis=1).astype(o_ref.dtype)


def _attn_prompt(bias_row, qt, kb, vt, *, tq, n_str):
    dq, t = qt.shape
    nb = t // tq
    n_kv = kb.shape[1] // HEAD_DIM
    q_per_kv = dq // HEAD_DIM // n_kv
    n = q_per_kv * tq
    assert vt.shape == (nb, n_kv * HEAD_DIM, tq) and bias_row.shape == (n_kv, 1, n) and n_kv % n_str == 0
    per_stream = lambda shape, dt: pltpu.VMEM((n_str,) + shape, dt)
    return pl.pallas_call(
        functools.partial(_attn_prompt_kernel, tq=tq, q_per_kv=q_per_kv, n_str=n_str),
        grid=(n_kv // n_str, nb),
        in_specs=[pl.BlockSpec((n_str, 1, n), lambda h, i: (h, 0, 0)),
                  pl.BlockSpec((n_str * q_per_kv * HEAD_DIM, tq), lambda h, i: (h, i)),
                  pl.BlockSpec((t, n_str * HEAD_DIM), lambda h, i: (0, h)),
                  pl.BlockSpec((nb, n_str * HEAD_DIM, tq), lambda h, i: (0, h, 0))],
        out_specs=pl.BlockSpec((tq, n_str * q_per_kv * HEAD_DIM), lambda h, i: (i, h)),
        out_shape=jax.ShapeDtypeStruct((t, dq), BF16),
        scratch_shapes=[per_stream((HEAD_DIM, n), F32), per_stream((1, n), F32), per_stream((2 * HEAD_DIM, n), BF16)]
        + [per_stream((tq, n), F32), per_stream((tq, n), BF16)] * 2,
        compiler_params=_cparams("parallel", "arbitrary"), name="attn_prompt",
    )(bias_row, qt, kb, vt)


def _attn_sample_kernel(pt_ref, q_ref, bias_ref, kn_ref, vn_ref, *refs, n_step, n_pages, n_sub):
    n_refs = n_sub * n_pages
    kc_refs, vc_refs, o_ref = refs[:n_refs], refs[n_refs:2 * n_refs], refs[2 * n_refs]
    rows = q_ref.shape[1]
    n_kv = kn_ref.shape[2] // HEAD_DIM
    page = kc_refs[0].shape[1] // n_kv
    rows_per_kv = rows // n_kv
    row_kv = lax.broadcasted_iota(jnp.int32, (rows, HEAD_DIM), 0) // rows_per_kv
    bias = bias_ref[...] * LOG2E

    def page_2d(ref):
        heads = [ref[0, pl.ds(h, page, stride=n_kv), :] for h in range(n_kv)]
        return jnp.concatenate(heads, axis=1).astype(BF16)

    def new_tokens(ref, s):
        pad = jnp.zeros((page - n_step, n_kv * HEAD_DIM), F32)
        return jnp.concatenate([ref[s], pad], axis=0).astype(BF16)

    def page_pair(page_refs, s, m):
        return jnp.concatenate([page_2d(page_refs[s * n_pages + 2 * m]),
                                page_2d(page_refs[s * n_pages + 2 * m + 1])], axis=0)

    def score(q_bd, k2, mask):
        s = lax.dot_general(q_bd, k2, (((1,), (1,)), ((), ())), preferred_element_type=F32)
        return _sb_scores(s + bias, mask)

    step = lax.broadcasted_iota(jnp.int32, (rows, page), 0) % n_step
    key = lax.broadcasted_iota(jnp.int32, (rows, page), 1)
    pairs = list(reversed(range(n_pages // 2)))
    u1, u2 = _suffix_ones(page), _suffix_ones(2 * page)
    scored = []
    for s in range(n_sub):
        q = q_ref[s]
        q_bd = jnp.concatenate([jnp.where(row_kv == h, q, jnp.zeros_like(q)) for h in range(n_kv)], axis=1)
        scored.append([score(q_bd, new_tokens(kn_ref, s), key < step)]
                      + [score(q_bd, page_pair(kc_refs, s, m), None) for m in pairs])
    sums = [[_dot(sp, u1 if i == 0 else u2) for i, (_, sp) in enumerate(blocks)] for blocks in scored]
    for s in range(n_sub):
        values = [new_tokens(vn_ref, s)] + [page_pair(vc_refs, s, m) for m in pairs]
        acc = jnp.zeros((rows, n_kv * HEAD_DIM), F32)
        carry = jnp.zeros((rows, 1), F32)
        for (z, _), c_local, v2 in zip(scored[s], sums[s], values):
            c = c_local + carry
            acc = acc + _dot(_sb_weights(z, c), v2)
            carry = c[:, :1]
        o = jnp.zeros((rows, HEAD_DIM), F32)
        for h in range(n_kv):
            o = o + jnp.where(row_kv == h, acc[:, h * HEAD_DIM:(h + 1) * HEAD_DIM], 0.0)
        o_ref[s] = o.astype(o_ref.dtype)


def _page_index(seq, n_sub, p, b, pt):
    return (pt[b * n_sub + seq, p], 0, 0)


def _attn_sample(page_table, q, bias_col, cache_k, cache_v, k_new, v_new, *, n_sub):
    n_seq, rows, _ = q.shape
    n_pages = page_table.shape[1]
    _, page, n_kv, _ = cache_k.shape
    n_step = k_new.shape[1]
    assert page % SUBLANES == 0 and n_step <= page and n_pages % 2 == 0 and n_seq % n_sub == 0
    per_seq = lambda shape: pl.BlockSpec((n_sub,) + shape, lambda b, pt: (b,) + (0,) * len(shape))
    cache_k = cache_k.reshape(-1, page * n_kv, HEAD_DIM)
    cache_v = cache_v.reshape(-1, page * n_kv, HEAD_DIM)
    pages = [pl.BlockSpec((1, page * n_kv, HEAD_DIM), functools.partial(_page_index, s, n_sub, p))
             for s in range(n_sub) for p in range(n_pages)]
    grid_spec = pltpu.PrefetchScalarGridSpec(
        num_scalar_prefetch=1, grid=(n_seq // n_sub,),
        in_specs=[per_seq((rows, HEAD_DIM)), pl.BlockSpec((rows, 1), lambda b, pt: (0, 0)),
                  per_seq((n_step, n_kv * HEAD_DIM)), per_seq((n_step, n_kv * HEAD_DIM))] + pages + pages,
        out_specs=per_seq((rows, HEAD_DIM)))
    return pl.pallas_call(
        functools.partial(_attn_sample_kernel, n_step=n_step, n_pages=n_pages, n_sub=n_sub),
        grid_spec=grid_spec, out_shape=jax.ShapeDtypeStruct((n_seq, rows, HEAD_DIM), BF16),
        compiler_params=_cparams("parallel"), name="attn_sample",
    )(page_table, q, bias_col, k_new, v_new, *([cache_k] * len(pages)), *([cache_v] * len(pages)))


def kernel(x_prompt, x_sample, p_prompt, p_sample, state_ssm_re, state_ssm_im, cache_k, cache_v, page_table, s5_lam_re, s5_lam_im, s5_log_dt, s5_b_re, s5_b_im, s5_c_re, s5_c_im, s5_d, s5_w_gate, s5_w_out, w_k, w_v, sb_w_q, sb_w_o, sb_bias, ln_mix_g, ln_mix_b, mlp_w1, mlp_w2, ln_mlp_g, ln_mlp_b, ple_w_proj, ple_w_gate, ln_ple_g, ln_ple_b):
    _, t_p, dm = x_prompt.shape
    n_seq, n_step, _ = x_sample.shape
    n_grp_total, p_state = s5_lam_re.shape[1:]
    n_blk = dm // CH_BLOCK
    gpb = CH_BLOCK // SSM_GROUP
    ns = gpb * p_state
    n_heads = sb_w_q.shape[2] // HEAD_DIM
    rows_s = n_seq * n_step
    assert x_prompt.shape[0] == 1 and n_step == SUBLANES and n_seq % SUBLANES == 0
    bf = lambda a: a.astype(BF16)
    vec = lambda a: a.reshape(1, -1)

    def to_group_order(a):
        c = a.shape[-1]
        return a.reshape(n_seq // SUBLANES, SUBLANES, n_step, c).transpose(0, 2, 1, 3).reshape(rows_s, c)

    def to_batch_order(a):
        c = a.shape[-1]
        return a.reshape(n_seq // SUBLANES, n_step, SUBLANES, c).transpose(0, 2, 1, 3).reshape(n_seq, n_step, c)

    xp = x_prompt.reshape(t_p, dm)
    xs = to_group_order(x_sample)
    pp = [p_prompt[i].reshape(t_p, -1) for i in range(DEPTH)]
    ps = [to_group_order(p_sample[i]) for i in range(DEPTH)]

    w_gate, w_out, w_o = bf(s5_w_gate[0]), bf(s5_w_out[0]), bf(sb_w_o[0])
    wp, wg = bf(ple_w_proj), bf(ple_w_gate)
    wq, wk, wv = bf(sb_w_q[0]), bf(w_k), bf(w_v)

    def glu(g, x):
        return _glu(g, x, w_gate, w_out, vec(ln_mix_g[0]), vec(ln_mix_b[0]), tm=256)

    def attn_out(o, x):
        return _attn_out(o, x, w_o, vec(ln_mix_g[1]), vec(ln_mix_b[1]), tm=512)

    def mlp(x, i):
        return _mlp(x, mlp_w1, mlp_w2, i, vec(ln_mlp_g[i]), vec(ln_mlp_b[i]), tm=1024, tf=512)

    def ple(x, p, i):
        return _ple(x, p, wp, wg, i, vec(ln_ple_g[i]), vec(ln_ple_b[i]), tm=512)

    ab_re, ab_im, bb_re, bb_im = _s5_prep(s5_lam_re[0], s5_lam_im[0], s5_log_dt[0], s5_b_re[0], s5_b_im[0])
    wb = bf(jnp.concatenate([_block_diag_in(bb_re, n_blk, gpb, p_state),
                             _block_diag_in(bb_im, n_blk, gpb, p_state)], axis=-1))
    wc_re = bf(_block_diag_out(s5_c_re[0], n_blk, gpb, p_state))
    wc_im = bf(_block_diag_out(s5_c_im[0], n_blk, gpb, p_state))
    a_re = ab_re.reshape(n_blk, ns)
    a_im = ab_im.reshape(n_blk, ns)
    d = vec(s5_d[0])
    g_p, hp_re, hp_im = _scan_prompt(xp, wb, wc_re, wc_im, a_re, a_im, d, tb=128)
    g_s, hs_re, hs_im = _scan_sample(xs, wb, wc_re, wc_im, a_re, a_im, d,
                                     state_ssm_re[0].reshape(n_seq, -1), state_ssm_im[0].reshape(n_seq, -1),
                                     n_step=n_step)
    xp = ple(mlp(glu(g_p, xp), 0), pp[0], 0)
    xs = ple(mlp(glu(g_s, xs), 0), ps[0], 0)

    tq = 256
    q_per_kv = n_heads // N_KV_HEADS
    qt, k_p, v_p, kb, vt = _qkv_prompt(xp, wq.T, wk, wv, wv.T, tm=tq)
    bias_row = jnp.repeat(sb_bias[0].reshape(N_KV_HEADS, q_per_kv), tq, axis=1)
    o_p = _attn_prompt(bias_row.reshape(N_KV_HEADS, 1, q_per_kv * tq), qt, kb, vt, tq=tq, n_str=2)
    q_s, k_s, v_s = _qkv_sample(xs, wq, wk, wv, tm=256)
    k_s = to_batch_order(k_s)
    v_s = to_batch_order(v_s)
    q_s = to_batch_order(q_s).reshape(n_seq, n_step, n_heads, HEAD_DIM).transpose(0, 2, 1, 3)
    bias_col = jnp.repeat(sb_bias[0], n_step).reshape(n_heads * n_step, 1)
    o_s = _attn_sample(page_table, q_s.reshape(n_seq, n_heads * n_step, HEAD_DIM), bias_col,
                       cache_k, cache_v, k_s, v_s, n_sub=1)
    o_s = o_s.reshape(n_seq, n_heads, n_step, HEAD_DIM).transpose(0, 2, 1, 3).reshape(n_seq, n_step, -1)
    xp = ple(mlp(attn_out(o_p, xp), 1), pp[1], 1)
    xs = ple(mlp(attn_out(to_group_order(o_s), xs), 1), ps[1], 1)

    state = lambda a: a.reshape(1, -1, n_grp_total, p_state)
    kv = lambda a, lead: a.reshape(lead + (N_KV_HEADS, HEAD_DIM))
    return (xp.reshape(1, t_p, dm), to_batch_order(xs),
            state(hp_re), state(hp_im), kv(k_p, (1, t_p)), kv(v_p, (1, t_p)),
            state(hs_re), state(hs_im), kv(k_s, (n_seq, n_step)), kv(v_s, (n_seq, n_step)))
```

```python
import functools
import math

import jax
import jax.numpy as jnp
from jax import lax
from jax.experimental import pallas as pl
from jax.experimental.pallas import tpu as pltpu

F32 = jnp.float32
BF16 = jnp.bfloat16

SSM_GROUP = 16
HEAD_DIM = 128
N_KV_HEADS = 4
LN_EPS = 1e-5
DEPTH = 2
DN_ALPHA = (2.0 * DEPTH) ** 0.25
LOG2E = 1.4426950408889634
QUERY_SCALE = HEAD_DIM ** -0.5 * LOG2E

SUBLANES = 8
LANES = 128
CH_BLOCK = 256
STEP_PITCH = 17
BLOCK_PITCH = 2
VMEM_LIMIT = 56 * 1024 * 1024
NORM_ROWS = 256


def _cparams(*sem):
    return pltpu.CompilerParams(dimension_semantics=sem, vmem_limit_bytes=VMEM_LIMIT)


def _dot(a, b):
    return jnp.dot(a, b, preferred_element_type=F32)


def _sigmoid(x):
    return 1.0 / (1.0 + jnp.exp(-x))


def _layer_norm_rows(y, g, b):
    mu = jnp.mean(y, axis=-1, keepdims=True)
    yc = y - mu
    var = jnp.mean(yc * yc, axis=-1, keepdims=True)
    return yc * lax.rsqrt(var + LN_EPS) * g + b


def _discretize(lam_re, lam_im, dt):
    mag = jnp.exp(lam_re * dt)
    ang = lam_im * dt
    ab_re = mag * jnp.cos(ang)
    ab_im = mag * jnp.sin(ang)
    den = lam_re * lam_re + lam_im * lam_im
    nr = ab_re - 1.0
    f_re = (nr * lam_re + ab_im * lam_im) / den
    f_im = (ab_im * lam_re - nr * lam_im) / den
    return ab_re, ab_im, f_re, f_im


def _s5_prep_kernel(lre_ref, lim_ref, ldt_ref, lre_rep_ref, lim_rep_ref, bre_ref, bim_ref,
                    abre_ref, abim_ref, bbre_ref, bbim_ref):
    dt = jnp.exp(ldt_ref[...])
    ab_re, ab_im, _, _ = _discretize(lre_ref[...], lim_ref[...], dt)
    abre_ref[...] = ab_re
    abim_ref[...] = ab_im
    _, _, f_re, f_im = _discretize(lre_rep_ref[...], lim_rep_ref[...], dt)
    b_re = bre_ref[...]
    b_im = bim_ref[...]
    bbre_ref[...] = f_re * b_re - f_im * b_im
    bbim_ref[...] = f_re * b_im + f_im * b_re


def _s5_prep(lam_re, lam_im, log_dt, b_re, b_im):
    g, p = lam_re.shape
    rep = lambda a: jnp.repeat(a, SSM_GROUP, axis=1)
    out = [jax.ShapeDtypeStruct((g, p), F32)] * 2 + [jax.ShapeDtypeStruct((g, p * SSM_GROUP), F32)] * 2
    return pl.pallas_call(_s5_prep_kernel, out_shape=out, name="s5_prep")(
        lam_re, lam_im, log_dt.reshape(g, 1), rep(lam_re), rep(lam_im),
        b_re.reshape(g, p * SSM_GROUP), b_im.reshape(g, p * SSM_GROUP))


def _block_diag(v, width):
    n_blk, rows, _ = v.shape
    gpb = CH_BLOCK // SSM_GROUP
    tiled = jnp.tile(v, (1, 1, gpb))
    row_grp = lax.broadcasted_iota(jnp.int32, (rows, gpb * width), 0) // (rows // gpb)
    col_grp = lax.broadcasted_iota(jnp.int32, (rows, gpb * width), 1) // width
    return jnp.where((row_grp == col_grp)[None], tiled, 0.0).astype(BF16)


def _block_diag_in(bb, n_blk, gpb, p):
    v = bb.reshape(n_blk, gpb, p, SSM_GROUP).transpose(0, 1, 3, 2)
    return _block_diag(v.reshape(n_blk, gpb * SSM_GROUP, p), p)


def _block_diag_out(c, n_blk, gpb, p):
    v = c.reshape(n_blk, gpb, SSM_GROUP, p).transpose(0, 1, 3, 2)
    return _block_diag(v.reshape(n_blk, gpb * p, SSM_GROUP), SSM_GROUP)


def _complex_step(a_re, a_im, h_re, h_im, b_re, b_im):
    return (a_re * h_re - a_im * h_im + b_re, a_re * h_im + a_im * h_re + b_im)


def _scan_prompt_kernel(x_ref, wb_ref, wcre_ref, wcim_ref, are_ref, aim_ref, d_ref,
                        g_ref, hre_ref, him_ref, s_ref, h_ref, *, n_blk, tb, ns):
    nc = ns // LANES
    block_rows = lambda j: pl.ds(j * BLOCK_PITCH, tb, stride=STEP_PITCH)

    @pl.when(pl.program_id(0) == 0)
    def _():
        h_ref[...] = jnp.zeros_like(h_ref)

    x = x_ref[...]
    for j in range(n_blk):
        u = x[:, j * CH_BLOCK:(j + 1) * CH_BLOCK].astype(BF16)
        bu = _dot(u, wb_ref[j])
        for c in range(2 * nc):
            s_ref[c, block_rows(j), :] = bu[:, c * LANES:(c + 1) * LANES]
    a_re = are_ref[...]
    a_im = aim_ref[...]

    def body(t, carry):
        rows = pl.ds(t * STEP_PITCH, n_blk, stride=BLOCK_PITCH)
        b_re = jnp.concatenate([s_ref[c, rows, :] for c in range(nc)], axis=1)
        b_im = jnp.concatenate([s_ref[nc + c, rows, :] for c in range(nc)], axis=1)
        h_re, h_im = _complex_step(a_re, a_im, carry[0], carry[1], b_re, b_im)
        for c in range(nc):
            s_ref[c, rows, :] = h_re[:, c * LANES:(c + 1) * LANES]
            s_ref[nc + c, rows, :] = h_im[:, c * LANES:(c + 1) * LANES]
        return h_re, h_im

    h_re, h_im = lax.fori_loop(0, tb, body, (h_ref[:, :ns], h_ref[:, ns:]), unroll=8)
    h_ref[...] = jnp.concatenate([h_re, h_im], axis=1)
    hre_ref[...] = h_re
    him_ref[...] = h_im

    ys = []
    for j in range(n_blk):
        hj_re = jnp.concatenate([s_ref[c, block_rows(j), :] for c in range(nc)], axis=1)
        hj_im = jnp.concatenate([s_ref[nc + c, block_rows(j), :] for c in range(nc)], axis=1)
        ys.append(_dot(hj_re.astype(BF16), wcre_ref[j]) - _dot(hj_im.astype(BF16), wcim_ref[j]))
    y = jnp.concatenate(ys, axis=1) + d_ref[...] * x
    g_ref[...] = jax.nn.gelu(y).astype(g_ref.dtype)


def _scan_prompt(x, wb, wc_re, wc_im, a_re, a_im, d, *, tb):
    t, dm = x.shape
    n_blk, _, ns2 = wb.shape
    ns = ns2 // 2
    assert n_blk == SUBLANES and t % tb == 0
    assert (n_blk - 1) * BLOCK_PITCH < STEP_PITCH
    full = lambda shape: pl.BlockSpec(shape, lambda i: (0,) * len(shape))
    const = lambda shape: pl.BlockSpec(shape, lambda i: (0,) * len(shape), pipeline_mode=pl.Buffered(1))
    return pl.pallas_call(
        functools.partial(_scan_prompt_kernel, n_blk=n_blk, tb=tb, ns=ns),
        grid=(t // tb,),
        in_specs=[pl.BlockSpec((tb, dm), lambda i: (i, 0)), const(wb.shape), const(wc_re.shape),
                  const(wc_im.shape), full(a_re.shape), full(a_im.shape), full(d.shape)],
        out_specs=[pl.BlockSpec((tb, dm), lambda i: (i, 0)), full((n_blk, ns)), full((n_blk, ns))],
        out_shape=[jax.ShapeDtypeStruct((t, dm), BF16), jax.ShapeDtypeStruct((n_blk, ns), F32),
                   jax.ShapeDtypeStruct((n_blk, ns), F32)],
        scratch_shapes=[pltpu.VMEM((ns2 // LANES, tb * STEP_PITCH, LANES), F32), pltpu.VMEM((n_blk, ns2), F32)],
        compiler_params=_cparams("arbitrary"),
        name="s5_scan_prompt",
    )(x, wb, wc_re, wc_im, a_re, a_im, d)


def _scan_sample_kernel(x_ref, wb_ref, wcre_ref, wcim_ref, are_ref, aim_ref, d_ref, h0re_ref, h0im_ref,
                        g_ref, hre_ref, him_ref, s_ref, *, n_grp, n_step, ns):
    x = x_ref[...]
    s_ref[...] = _dot(x.astype(BF16), wb_ref[0])
    a_re = jnp.broadcast_to(are_ref[0], (SUBLANES, ns))
    a_im = jnp.broadcast_to(aim_ref[0], (SUBLANES, ns))

    def body(bg, _):
        b0 = pl.multiple_of(bg * SUBLANES, SUBLANES)
        h_re = h0re_ref[pl.ds(b0, SUBLANES), :]
        h_im = h0im_ref[pl.ds(b0, SUBLANES), :]
        for t in range(n_step):
            r0 = pl.multiple_of((bg * n_step + t) * SUBLANES, SUBLANES)
            tile = s_ref[pl.ds(r0, SUBLANES), :]
            h_re, h_im = _complex_step(a_re, a_im, h_re, h_im, tile[:, :ns], tile[:, ns:])
            s_ref[pl.ds(r0, SUBLANES), :] = jnp.concatenate([h_re, h_im], axis=1)
        hre_ref[pl.ds(b0, SUBLANES), :] = h_re
        him_ref[pl.ds(b0, SUBLANES), :] = h_im
        return 0

    lax.fori_loop(0, n_grp, body, 0)
    h = s_ref[...]
    y = _dot(h[:, :ns].astype(BF16), wcre_ref[0]) - _dot(h[:, ns:].astype(BF16), wcim_ref[0])
    g_ref[...] = jax.nn.gelu(y + d_ref[...] * x).astype(g_ref.dtype)


def _scan_sample(x, wb, wc_re, wc_im, a_re, a_im, d, h0_re, h0_im, *, n_step):
    rows, dm = x.shape
    n_blk, _, ns2 = wb.shape
    ns = ns2 // 2
    n_seq = h0_re.shape[0]
    assert rows == n_seq * n_step and n_seq % SUBLANES == 0
    col = lambda shape: pl.BlockSpec(shape, lambda j: (0, j))
    lead = lambda shape: pl.BlockSpec(shape, lambda j: (j,) + (0,) * (len(shape) - 1))
    return pl.pallas_call(
        functools.partial(_scan_sample_kernel, n_grp=n_seq // SUBLANES, n_step=n_step, ns=ns),
        grid=(n_blk,),
        in_specs=[col((rows, CH_BLOCK)), lead((1, CH_BLOCK, ns2)), lead((1, ns, CH_BLOCK)),
                  lead((1, ns, CH_BLOCK)), lead((1, 1, ns)), lead((1, 1, ns)), col((1, CH_BLOCK)),
                  col((n_seq, ns)), col((n_seq, ns))],
        out_specs=[col((rows, CH_BLOCK)), col((n_seq, ns)), col((n_seq, ns))],
        out_shape=[jax.ShapeDtypeStruct((rows, dm), BF16), jax.ShapeDtypeStruct((n_seq, n_blk * ns), F32),
                   jax.ShapeDtypeStruct((n_seq, n_blk * ns), F32)],
        scratch_shapes=[pltpu.VMEM((rows, ns2), F32)],
        compiler_params=_cparams("arbitrary"),
        name="s5_scan_sample",
    )(x, wb, wc_re, wc_im, a_re.reshape(n_blk, 1, ns), a_im.reshape(n_blk, 1, ns), d, h0_re, h0_im)


def _post_norm_store(o_ref, x, sub, g_ref, b_ref):
    o_ref[...] = _layer_norm_rows(DN_ALPHA * x + sub, g_ref[...], b_ref[...])


def _glu_kernel(a_ref, x_ref, w1_ref, w2_ref, g_ref, b_ref, o_ref):
    a = a_ref[...]
    h = a.astype(F32) * _sigmoid(_dot(a, w1_ref[...]))
    _post_norm_store(o_ref, x_ref[...], _dot(h.astype(BF16), w2_ref[...]), g_ref, b_ref)


def _glu(a, x, w1, w2, g, b, *, tm):
    t, dm = x.shape
    row = pl.BlockSpec((tm, dm), lambda i: (i, 0))
    const = lambda arr: pl.BlockSpec(arr.shape, lambda i: (0, 0), pipeline_mode=pl.Buffered(1))
    return pl.pallas_call(
        _glu_kernel, grid=(t // tm,),
        in_specs=[row, row, const(w1), const(w2), const(g), const(b)],
        out_specs=row, out_shape=jax.ShapeDtypeStruct((t, dm), F32),
        compiler_params=_cparams("parallel"), name="glu",
    )(a, x, w1, w2, g, b)


def _mlp_kernel(x_ref, w1_ref, w2_ref, g_ref, b_ref, o_ref, xb_ref):
    f = pl.program_id(1)

    @pl.when(f == 0)
    def _():
        xb_ref[...] = x_ref[...].astype(BF16)
        o_ref[...] = jnp.zeros_like(o_ref)

    h = jnp.square(jnp.maximum(_dot(xb_ref[...], w1_ref[...].astype(BF16)), 0.0))
    o_ref[...] += _dot(h.astype(BF16), w2_ref[...].astype(BF16))

    @pl.when(f == pl.num_programs(1) - 1)
    def _():
        for r0 in range(0, o_ref.shape[0], NORM_ROWS):
            rows = slice(r0, r0 + NORM_ROWS)
            o_ref[rows] = _layer_norm_rows(DN_ALPHA * x_ref[rows] + o_ref[rows], g_ref[...], b_ref[...])


def _mlp(x, w1, w2, layer, g, b, *, tm, tf):
    t, dm = x.shape
    once = pl.Buffered(1)
    row = pl.BlockSpec((tm, dm), lambda i, f: (i, 0), pipeline_mode=once)
    vec = pl.BlockSpec((1, dm), lambda i, f: (0, 0), pipeline_mode=once)
    w1s = pl.BlockSpec((None, dm, tf), lambda i, f: (layer, 0, f))
    w2s = pl.BlockSpec((None, tf, dm), lambda i, f: (layer, f, 0))
    return pl.pallas_call(
        _mlp_kernel, grid=(t // tm, w1.shape[2] // tf),
        in_specs=[row, w1s, w2s, vec, vec],
        out_specs=pl.BlockSpec((tm, dm), lambda i, f: (i, 0)),
        out_shape=jax.ShapeDtypeStruct((t, dm), F32),
        scratch_shapes=[pltpu.VMEM((tm, dm), BF16)],
        compiler_params=_cparams("parallel", "arbitrary"), name="mlp",
    )(x, w1, w2, g, b)


def _ple_kernel(x_ref, p_ref, wp_ref, wg_ref, g_ref, b_ref, o_ref):
    x = x_ref[...]
    sub = _dot(p_ref[...].astype(BF16), wp_ref[...]) * _sigmoid(_dot(x.astype(BF16), wg_ref[...]))
    _post_norm_store(o_ref, x, sub, g_ref, b_ref)


def _ple(x, p, wp, wg, layer, g, b, *, tm):
    t, dm = x.shape
    row = pl.BlockSpec((tm, dm), lambda i: (i, 0))
    full = lambda a: pl.BlockSpec(a.shape, lambda i: (0, 0))
    of_layer = lambda a: pl.BlockSpec((None,) + a.shape[1:], lambda i: (layer, 0, 0),
                                      pipeline_mode=pl.Buffered(1))
    return pl.pallas_call(
        _ple_kernel, grid=(t // tm,),
        in_specs=[row, pl.BlockSpec((tm, p.shape[1]), lambda i: (i, 0)), of_layer(wp), of_layer(wg),
                  full(g), full(b)],
        out_specs=row, out_shape=jax.ShapeDtypeStruct((t, dm), F32),
        compiler_params=_cparams("parallel"), name="ple",
    )(x, p, wp, wg, g, b)


def _attn_out_kernel(o_ref, x_ref, w_ref, g_ref, b_ref, y_ref):
    _post_norm_store(y_ref, x_ref[...], _dot(o_ref[...], w_ref[...]), g_ref, b_ref)


def _attn_out(o, x, w, g, b, *, tm):
    t, dm = x.shape
    row = pl.BlockSpec((tm, dm), lambda i: (i, 0))
    full = lambda a: pl.BlockSpec(a.shape, lambda i: (0, 0))
    return pl.pallas_call(
        _attn_out_kernel, grid=(t // tm,),
        in_specs=[row, row, full(w), full(g), full(b)],
        out_specs=row, out_shape=jax.ShapeDtypeStruct((t, dm), F32),
        compiler_params=_cparams("parallel"), name="attn_out",
    )(o, x, w, g, b)


def _dot_t(w_t, x):
    return lax.dot_general(w_t, x, (((1,), (1,)), ((), ())), preferred_element_type=F32)


def _qkv_prompt_kernel(x_ref, wqt_ref, wk_ref, wv_ref, wvt_ref, qt_ref, k_ref, v_ref, kb_ref, vt_ref):
    tm = x_ref.shape[0]
    n_kv = wk_ref.shape[1] // HEAD_DIM
    xb = x_ref[...].astype(BF16)
    qt_ref[...] = (_dot_t(wqt_ref[...], xb) * QUERY_SCALE).astype(qt_ref.dtype)
    k = _dot(xb, wk_ref[...])
    v = _dot(xb, wv_ref[...])
    kb_ref[...] = k.astype(kb_ref.dtype)
    vt_ref[0] = _dot_t(wvt_ref[...], xb).astype(vt_ref.dtype)
    for h in range(n_kv):
        k_ref[pl.ds(h, tm, stride=n_kv), :] = k[:, h * HEAD_DIM:(h + 1) * HEAD_DIM]
        v_ref[pl.ds(h, tm, stride=n_kv), :] = v[:, h * HEAD_DIM:(h + 1) * HEAD_DIM]


def _qkv_prompt(x, wqt, wk, wv, wvt, *, tm):
    t, dm = x.shape
    dq, dkv = wqt.shape[0], wk.shape[1]
    n_kv = dkv // HEAD_DIM
    row = lambda n: pl.BlockSpec((tm, n), lambda i: (i, 0))
    head_rows = pl.BlockSpec((tm * n_kv, HEAD_DIM), lambda i: (i, 0))
    const = lambda a: pl.BlockSpec(a.shape, lambda i: (0, 0), pipeline_mode=pl.Buffered(1))
    return pl.pallas_call(
        _qkv_prompt_kernel, grid=(t // tm,),
        in_specs=[row(dm), const(wqt), const(wk), const(wv), const(wvt)],
        out_specs=[pl.BlockSpec((dq, tm), lambda i: (0, i)), head_rows, head_rows, row(dkv),
                   pl.BlockSpec((1, dkv, tm), lambda i: (i, 0, 0))],
        out_shape=[jax.ShapeDtypeStruct((dq, t), BF16), jax.ShapeDtypeStruct((t * n_kv, HEAD_DIM), F32),
                   jax.ShapeDtypeStruct((t * n_kv, HEAD_DIM), F32), jax.ShapeDtypeStruct((t, dkv), BF16),
                   jax.ShapeDtypeStruct((t // tm, dkv, tm), BF16)],
        compiler_params=_cparams("parallel"), name="qkv_prompt",
    )(x, wqt, wk, wv, wvt)


def _qkv_sample_kernel(x_ref, wq_ref, wk_ref, wv_ref, q_ref, k_ref, v_ref):
    xb = x_ref[...].astype(BF16)
    q_ref[...] = (_dot(xb, wq_ref[...]) * QUERY_SCALE).astype(q_ref.dtype)
    k_ref[...] = _dot(xb, wk_ref[...])
    v_ref[...] = _dot(xb, wv_ref[...])


def _qkv_sample(x, wq, wk, wv, *, tm):
    t, dm = x.shape
    dq, dkv = wq.shape[1], wk.shape[1]
    row = lambda n: pl.BlockSpec((tm, n), lambda i: (i, 0))
    const = lambda a: pl.BlockSpec(a.shape, lambda i: (0, 0), pipeline_mode=pl.Buffered(1))
    return pl.pallas_call(
        _qkv_sample_kernel, grid=(t // tm,),
        in_specs=[row(dm), const(wq), const(wk), const(wv)],
        out_specs=[row(dq), row(dkv), row(dkv)],
        out_shape=[jax.ShapeDtypeStruct((t, dq), BF16), jax.ShapeDtypeStruct((t, dkv), F32),
                   jax.ShapeDtypeStruct((t, dkv), F32)],
        compiler_params=_cparams("parallel"), name="qkv_sample",
    )(x, wq, wk, wv)


MASKED_LOGIT = -1e30

STEPS_PER_TRIP = 4

EXP2_CLAMP = 126.0


def _softplus2(z):
    return jnp.maximum(jnp.log(1.0 + jnp.exp2(jnp.minimum(z, EXP2_CLAMP))) * LOG2E, z)


def _suffix_ones(n):
    return (lax.broadcasted_iota(jnp.int32, (n, n), 0) >= lax.broadcasted_iota(jnp.int32, (n, n), 1)).astype(BF16)


def _suffix_ones_t(n):
    return (lax.broadcasted_iota(jnp.int32, (n, n), 1) >= lax.broadcasted_iota(jnp.int32, (n, n), 0)).astype(BF16)


def _sb_scores(z, mask):
    if mask is not None:
        z = jnp.where(mask, z, MASKED_LOGIT)
    return z, _softplus2(z).astype(BF16)


def _sb_weights(z, c):
    return jnp.exp2(z - c).astype(BF16)


def _attn_prompt_kernel(bias_ref, qt_ref, k_ref, vt_ref, o_ref, acc_ref, carry_ref,
                        qa_ref, z0_ref, sp0_ref, z1_ref, sp1_ref, *, tq, q_per_kv, n_str):
    qi = pl.program_id(1)
    n = q_per_kv * tq
    u = _suffix_ones_t(tq)
    streams = range(n_str)
    head = lambda j: slice(j * HEAD_DIM, (j + 1) * HEAD_DIM)
    row = lax.broadcasted_iota(jnp.int32, (HEAD_DIM, n), 0)
    for j in streams:
        b2 = bias_ref[j] * LOG2E
        b_hi = b2.astype(BF16).astype(F32)
        extra = jnp.where(row == 0, b_hi, jnp.where(row == 1, b2 - b_hi, 0.0))
        qa_ref[j, :HEAD_DIM] = jnp.concatenate([qt_ref[head(j * q_per_kv + g), :] for g in range(q_per_kv)], axis=1)
        qa_ref[j, HEAD_DIM:] = extra.astype(BF16)
    ones2 = (lax.broadcasted_iota(jnp.int32, (tq, HEAD_DIM), 1) < 2).astype(BF16)
    acc_ref[...] = jnp.zeros_like(acc_ref)
    carry_ref[...] = jnp.zeros_like(carry_ref)
    stage = ((z0_ref, sp0_ref), (z1_ref, sp1_ref))

    def scores(kj, j):
        kb = k_ref[pl.ds(pl.multiple_of(kj * tq, tq), tq), head(j)]
        return _dot(jnp.concatenate([kb, ones2], axis=1), qa_ref[j])

    def suffix_sums(slot, j):
        return _dot(u, stage[slot][1][j]) + carry_ref[j]

    def weigh(kj, slot, j, c):
        acc_ref[j] += _dot(vt_ref[kj, head(j), :], _sb_weights(stage[slot][0][j], c))
        carry_ref[j] = c[:1, :]

    def score(kj, slot, mask):
        z_ref, sp_ref = stage[slot]
        for j in streams:
            z_ref[j], sp_ref[j] = _sb_scores(scores(kj, j), mask)

    def accumulate(kj, slot):
        cs = [suffix_sums(slot, j) for j in streams]
        for j in streams:
            weigh(kj, slot, j, cs[j])

    def step(kj, slot):
        zn_ref, spn_ref = stage[1 - slot]
        cs = [suffix_sums(slot, j) for j in streams]
        ss = [scores(kj - 1, j) for j in streams]
        for j in streams:
            zn_ref[j], spn_ref[j] = _sb_scores(ss[j], None)
        for j in streams:
            weigh(kj, slot, j, cs[j])

    key = lax.broadcasted_iota(jnp.int32, (tq, n), 0)
    query = lax.broadcasted_iota(jnp.int32, (tq, n), 1) % tq
    score(qi, 0, key < query)

    def steps(kj, count):
        for i in range(count):
            step(kj - i, i % 2)

    def unrolled(m, _):
        steps(qi - STEPS_PER_TRIP * m, STEPS_PER_TRIP)
        return 0

    lax.fori_loop(0, qi // STEPS_PER_TRIP, unrolled, 0)
    left = qi % STEPS_PER_TRIP
    for r in range(STEPS_PER_TRIP):
        @pl.when(left == r)
        def _(r=r):
            steps(r, r)
            accumulate(0, r % 2)

    o_ref[...] = jnp.concatenate([acc_ref[j, :, g * tq:(g + 1) * tq].T for j in streams for g in range(q_per_kv)],
                                 axis=1).astype(o_ref.dtype)


def _attn_prompt(bias_row, qt, kb, vt, *, tq, n_str):
    dq, t = qt.shape
    nb = t // tq
    n_kv = kb.shape[1] // HEAD_DIM
    q_per_kv = dq // HEAD_DIM // n_kv
    n = q_per_kv * tq
    assert vt.shape == (nb, n_kv * HEAD_DIM, tq) and bias_row.shape == (n_kv, 1, n) and n_kv % n_str == 0
    per_stream = lambda shape, dt: pltpu.VMEM((n_str,) + shape, dt)
    return pl.pallas_call(
        functools.partial(_attn_prompt_kernel, tq=tq, q_per_kv=q_per_kv, n_str=n_str),
        grid=(n_kv // n_str, nb),
        in_specs=[pl.BlockSpec((n_str, 1, n), lambda h, i: (h, 0, 0)),
                  pl.BlockSpec((n_str * q_per_kv * HEAD_DIM, tq), lambda h, i: (h, i)),
                  pl.BlockSpec((t, n_str * HEAD_DIM), lambda h, i: (0, h)),
                  pl.BlockSpec((nb, n_str * HEAD_DIM, tq), lambda h, i: (0, h, 0))],
        out_specs=pl.BlockSpec((tq, n_str * q_per_kv * HEAD_DIM), lambda h, i: (i, h)),
        out_shape=jax.ShapeDtypeStruct((t, dq), BF16),
        scratch_shapes=[per_stream((HEAD_DIM, n), F32), per_stream((1, n), F32), per_stream((2 * HEAD_DIM, n), BF16)]
        + [per_stream((tq, n), F32), per_stream((tq, n), BF16)] * 2,
        compiler_params=_cparams("parallel", "arbitrary"), name="attn_prompt",
    )(bias_row, qt, kb, vt)


def _attn_sample_kernel(pt_ref, q_ref, bias_ref, kn_ref, vn_ref, *refs, n_step, n_pages, n_sub):
    n_refs = n_sub * n_pages
    kc_refs, vc_refs, o_ref = refs[:n_refs], refs[n_refs:2 * n_refs], refs[2 * n_refs]
    rows = q_ref.shape[1]
    n_kv = kn_ref.shape[2] // HEAD_DIM
    page = kc_refs[0].shape[1] // n_kv
    rows_per_kv = rows // n_kv
    row_kv = lax.broadcasted_iota(jnp.int32, (rows, HEAD_DIM), 0) // rows_per_kv
    bias = bias_ref[...] * LOG2E

    def page_2d(ref):
        heads = [ref[0, pl.ds(h, page, stride=n_kv), :] for h in range(n_kv)]
        return jnp.concatenate(heads, axis=1).astype(BF16)

    def new_tokens(ref, s):
        pad = jnp.zeros((page - n_step, n_kv * HEAD_DIM), F32)
        return jnp.concatenate([ref[s], pad], axis=0).astype(BF16)

    def page_pair(page_refs, s, m):
        return jnp.concatenate([page_2d(page_refs[s * n_pages + 2 * m]),
                                page_2d(page_refs[s * n_pages + 2 * m + 1])], axis=0)

    def score(q_bd, k2, mask):
        s = lax.dot_general(q_bd, k2, (((1,), (1,)), ((), ())), preferred_element_type=F32)
        return _sb_scores(s + bias, mask)

    step = lax.broadcasted_iota(jnp.int32, (rows, page), 0) % n_step
    key = lax.broadcasted_iota(jnp.int32, (rows, page), 1)
    pairs = list(reversed(range(n_pages // 2)))
    u1, u2 = _suffix_ones(page), _suffix_ones(2 * page)
    scored = []
    for s in range(n_sub):
        q = q_ref[s]
        q_bd = jnp.concatenate([jnp.where(row_kv == h, q, jnp.zeros_like(q)) for h in range(n_kv)], axis=1)
        scored.append([score(q_bd, new_tokens(kn_ref, s), key < step)]
                      + [score(q_bd, page_pair(kc_refs, s, m), None) for m in pairs])
    sums = [[_dot(sp, u1 if i == 0 else u2) for i, (_, sp) in enumerate(blocks)] for blocks in scored]
    for s in range(n_sub):
        values = [new_tokens(vn_ref, s)] + [page_pair(vc_refs, s, m) for m in pairs]
        acc = jnp.zeros((rows, n_kv * HEAD_DIM), F32)
        carry = jnp.zeros((rows, 1), F32)
        for (z, _), c_local, v2 in zip(scored[s], sums[s], values):
            c = c_local + carry
            acc = acc + _dot(_sb_weights(z, c), v2)
            carry = c[:, :1]
        o = jnp.zeros((rows, HEAD_DIM), F32)
        for h in range(n_kv):
            o = o + jnp.where(row_kv == h, acc[:, h * HEAD_DIM:(h + 1) * HEAD_DIM], 0.0)
        o_ref[s] = o.astype(o_ref.dtype)


def _page_index(seq, n_sub, p, b, pt):
    return (pt[b * n_sub + seq, p], 0, 0)


def _attn_sample(page_table, q, bias_col, cache_k, cache_v, k_new, v_new, *, n_sub):
    n_seq, rows, _ = q.shape
    n_pages = page_table.shape[1]
    _, page, n_kv, _ = cache_k.shape
    n_step = k_new.shape[1]
    assert page % SUBLANES == 0 and n_step <= page and n_pages % 2 == 0 and n_seq % n_sub == 0
    per_seq = lambda shape: pl.BlockSpec((n_sub,) + shape, lambda b, pt: (b,) + (0,) * len(shape))
    cache_k = cache_k.reshape(-1, page * n_kv, HEAD_DIM)
    cache_v = cache_v.reshape(-1, page * n_kv, HEAD_DIM)
    pages = [pl.BlockSpec((1, page * n_kv, HEAD_DIM), functools.partial(_page_index, s, n_sub, p))
             for s in range(n_sub) for p in range(n_pages)]
    grid_spec = pltpu.PrefetchScalarGridSpec(
        num_scalar_prefetch=1, grid=(n_seq // n_sub,),
        in_specs=[per_seq((rows, HEAD_DIM)), pl.BlockSpec((rows, 1), lambda b, pt: (0, 0)),
                  per_seq((n_step, n_kv * HEAD_DIM)), per_seq((n_step, n_kv * HEAD_DIM))] + pages + pages,
        out_specs=per_seq((rows, HEAD_DIM)))
    return pl.pallas_call(
        functools.partial(_attn_sample_kernel, n_step=n_step, n_pages=n_pages, n_sub=n_sub),
        grid_spec=grid_spec, out_shape=jax.ShapeDtypeStruct((n_seq, rows, HEAD_DIM), BF16),
        compiler_params=_cparams("parallel"), name="attn_sample",
    )(page_table, q, bias_col, k_new, v_new, *([cache_k] * len(pages)), *([cache_v] * len(pages)))


def kernel(x_prompt, x_sample, p_prompt, p_sample, state_ssm_re, state_ssm_im, cache_k, cache_v, page_table, s5_lam_re, s5_lam_im, s5_log_dt, s5_b_re, s5_b_im, s5_c_re, s5_c_im, s5_d, s5_w_gate, s5_w_out, w_k, w_v, sb_w_q, sb_w_o, sb_bias, ln_mix_g, ln_mix_b, mlp_w1, mlp_w2, ln_mlp_g, ln_mlp_b, ple_w_proj, ple_w_gate, ln_ple_g, ln_ple_b):
    _, t_p, dm = x_prompt.shape
    n_seq, n_step, _ = x_sample.shape
    n_grp_total, p_state = s5_lam_re.shape[1:]
    n_blk = dm // CH_BLOCK
    gpb = CH_BLOCK // SSM_GROUP
    ns = gpb * p_state
    n_heads = sb_w_q.shape[2] // HEAD_DIM
    rows_s = n_seq * n_step
    assert x_prompt.shape[0] == 1 and n_step == SUBLANES and n_seq % SUBLANES == 0
    bf = lambda a: a.astype(BF16)
    vec = lambda a: a.reshape(1, -1)

    def to_group_order(a):
        c = a.shape[-1]
        return a.reshape(n_seq // SUBLANES, SUBLANES, n_step, c).transpose(0, 2, 1, 3).reshape(rows_s, c)

    def to_batch_order(a):
        c = a.shape[-1]
        return a.reshape(n_seq // SUBLANES, n_step, SUBLANES, c).transpose(0, 2, 1, 3).reshape(n_seq, n_step, c)

    xp = x_prompt.reshape(t_p, dm)
    xs = to_group_order(x_sample)
    pp = [p_prompt[i].reshape(t_p, -1) for i in range(DEPTH)]
    ps = [to_group_order(p_sample[i]) for i in range(DEPTH)]

    w_gate, w_out, w_o = bf(s5_w_gate[0]), bf(s5_w_out[0]), bf(sb_w_o[0])
    wp, wg = bf(ple_w_proj), bf(ple_w_gate)
    wq, wk, wv = bf(sb_w_q[0]), bf(w_k), bf(w_v)

    def glu(g, x):
        return _glu(g, x, w_gate, w_out, vec(ln_mix_g[0]), vec(ln_mix_b[0]), tm=256)

    def attn_out(o, x):
        return _attn_out(o, x, w_o, vec(ln_mix_g[1]), vec(ln_mix_b[1]), tm=512)

    def mlp(x, i):
        return _mlp(x, mlp_w1, mlp_w2, i, vec(ln_mlp_g[i]), vec(ln_mlp_b[i]), tm=1024, tf=512)

    def ple(x, p, i):
        return _ple(x, p, wp, wg, i, vec(ln_ple_g[i]), vec(ln_ple_b[i]), tm=512)

    ab_re, ab_im, bb_re, bb_im = _s5_prep(s5_lam_re[0], s5_lam_im[0], s5_log_dt[0], s5_b_re[0], s5_b_im[0])
    wb = bf(jnp.concatenate([_block_diag_in(bb_re, n_blk, gpb, p_state),
                             _block_diag_in(bb_im, n_blk, gpb, p_state)], axis=-1))
    wc_re = bf(_block_diag_out(s5_c_re[0], n_blk, gpb, p_state))
    wc_im = bf(_block_diag_out(s5_c_im[0], n_blk, gpb, p_state))
    a_re = ab_re.reshape(n_blk, ns)
    a_im = ab_im.reshape(n_blk, ns)
    d = vec(s5_d[0])
    g_p, hp_re, hp_im = _scan_prompt(xp, wb, wc_re, wc_im, a_re, a_im, d, tb=128)
    g_s, hs_re, hs_im = _scan_sample(xs, wb, wc_re, wc_im, a_re, a_im, d,
                                     state_ssm_re[0].reshape(n_seq, -1), state_ssm_im[0].reshape(n_seq, -1),
                                     n_step=n_step)
    xp = ple(mlp(glu(g_p, xp), 0), pp[0], 0)
    xs = ple(mlp(glu(g_s, xs), 0), ps[0], 0)

    tq = 256
    q_per_kv = n_heads // N_KV_HEADS
    qt, k_p, v_p, kb, vt = _qkv_prompt(xp, wq.T, wk, wv, wv.T, tm=tq)
    bias_row = jnp.repeat(sb_bias[0].reshape(N_KV_HEADS, q_per_kv), tq, axis=1)
    o_p = _attn_prompt(bias_row.reshape(N_KV_HEADS, 1, q_per_kv * tq), qt, kb, vt, tq=tq, n_str=2)
    q_s, k_s, v_s = _qkv_sample(xs, wq, wk, wv, tm=256)
    k_s = to_batch_order(k_s)
    v_s = to_batch_order(v_s)
    q_s = to_batch_order(q_s).reshape(n_seq, n_step, n_heads, HEAD_DIM).transpose(0, 2, 1, 3)
    bias_col = jnp.repeat(sb_bias[0], n_step).reshape(n_heads * n_step, 1)
    o_s = _attn_sample(page_table, q_s.reshape(n_seq, n_heads * n_step, HEAD_DIM), bias_col,
                       cache_k, cache_v, k_s, v_s, n_sub=1)
    o_s = o_s.reshape(n_seq, n_heads, n_step, HEAD_DIM).transpose(0, 2, 1, 3).reshape(n_seq, n_step, -1)
    xp = ple(mlp(attn_out(o_p, xp), 1), pp[1], 1)
    xs = ple(mlp(attn_out(to_group_order(o_s), xs), 1), ps[1], 1)

    state = lambda a: a.reshape(1, -1, n_grp_total, p_state)
    kv = lambda a, lead: a.reshape(lead + (N_KV_HEADS, HEAD_DIM))
    return (xp.reshape(1, t_p, dm), to_batch_order(xs),
            state(hp_re), state(hp_im), kv(k_p, (1, t_p)), kv(v_p, (1, t_p)),
            state(hs_re), state(hs_im), kv(k_s, (n_seq, n_step)), kv(v_s, (n_seq, n_step)))
```

```python
import functools
import math

import jax
import jax.numpy as jnp
from jax import lax
from jax.experimental import pallas as pl
from jax.experimental.pallas import tpu as pltpu

F32 = jnp.float32
BF16 = jnp.bfloat16

SSM_GROUP = 16
HEAD_DIM = 128
N_KV_HEADS = 4
LN_EPS = 1e-5
DEPTH = 2
DN_ALPHA = (2.0 * DEPTH) ** 0.25
LOG2E = 1.4426950408889634
QUERY_SCALE = HEAD_DIM ** -0.5 * LOG2E

SUBLANES = 8
LANES = 128
CH_BLOCK = 256
STEP_PITCH = 17
BLOCK_PITCH = 2
VMEM_LIMIT = 56 * 1024 * 1024
NORM_ROWS = 256


def _cparams(*sem):
    return pltpu.CompilerParams(dimension_semantics=sem, vmem_limit_bytes=VMEM_LIMIT)


def _dot(a, b):
    return jnp.dot(a, b, preferred_element_type=F32)


def _sigmoid(x):
    return 1.0 / (1.0 + jnp.exp(-x))


def _layer_norm_rows(y, g, b):
    mu = jnp.mean(y, axis=-1, keepdims=True)
    yc = y - mu
    var = jnp.mean(yc * yc, axis=-1, keepdims=True)
    return yc * lax.rsqrt(var + LN_EPS) * g + b


def _discretize(lam_re, lam_im, dt):
    mag = jnp.exp(lam_re * dt)
    ang = lam_im * dt
    ab_re = mag * jnp.cos(ang)
    ab_im = mag * jnp.sin(ang)
    den = lam_re * lam_re + lam_im * lam_im
    nr = ab_re - 1.0
    f_re = (nr * lam_re + ab_im * lam_im) / den
    f_im = (ab_im * lam_re - nr * lam_im) / den
    return ab_re, ab_im, f_re, f_im


def _s5_prep_kernel(lre_ref, lim_ref, ldt_ref, lre_rep_ref, lim_rep_ref, bre_ref, bim_ref,
                    abre_ref, abim_ref, bbre_ref, bbim_ref):
    dt = jnp.exp(ldt_ref[...])
    ab_re, ab_im, _, _ = _discretize(lre_ref[...], lim_ref[...], dt)
    abre_ref[...] = ab_re
    abim_ref[...] = ab_im
    _, _, f_re, f_im = _discretize(lre_rep_ref[...], lim_rep_ref[...], dt)
    b_re = bre_ref[...]
    b_im = bim_ref[...]
    bbre_ref[...] = f_re * b_re - f_im * b_im
    bbim_ref[...] = f_re * b_im + f_im * b_re


def _s5_prep(lam_re, lam_im, log_dt, b_re, b_im):
    g, p = lam_re.shape
    rep = lambda a: jnp.repeat(a, SSM_GROUP, axis=1)
    out = [jax.ShapeDtypeStruct((g, p), F32)] * 2 + [jax.ShapeDtypeStruct((g, p * SSM_GROUP), F32)] * 2
    return pl.pallas_call(_s5_prep_kernel, out_shape=out, name="s5_prep")(
        lam_re, lam_im, log_dt.reshape(g, 1), rep(lam_re), rep(lam_im),
        b_re.reshape(g, p * SSM_GROUP), b_im.reshape(g, p * SSM_GROUP))


def _block_diag(v, width):
    n_blk, rows, _ = v.shape
    gpb = CH_BLOCK // SSM_GROUP
    tiled = jnp.tile(v, (1, 1, gpb))
    row_grp = lax.broadcasted_iota(jnp.int32, (rows, gpb * width), 0) // (rows // gpb)
    col_grp = lax.broadcasted_iota(jnp.int32, (rows, gpb * width), 1) // width
    return jnp.where((row_grp == col_grp)[None], tiled, 0.0).astype(BF16)


def _block_diag_in(bb, n_blk, gpb, p):
    v = bb.reshape(n_blk, gpb, p, SSM_GROUP).transpose(0, 1, 3, 2)
    return _block_diag(v.reshape(n_blk, gpb * SSM_GROUP, p), p)


def _block_diag_out(c, n_blk, gpb, p):
    v = c.reshape(n_blk, gpb, SSM_GROUP, p).transpose(0, 1, 3, 2)
    return _block_diag(v.reshape(n_blk, gpb * p, SSM_GROUP), SSM_GROUP)


def _complex_step(a_re, a_im, h_re, h_im, b_re, b_im):
    return (a_re * h_re - a_im * h_im + b_re, a_re * h_im + a_im * h_re + b_im)


def _scan_prompt_kernel(x_ref, wb_ref, wcre_ref, wcim_ref, are_ref, aim_ref, d_ref,
                        g_ref, hre_ref, him_ref, s_ref, h_ref, *, n_blk, tb, ns):
    nc = ns // LANES
    block_rows = lambda j: pl.ds(j * BLOCK_PITCH, tb, stride=STEP_PITCH)

    @pl.when(pl.program_id(0) == 0)
    def _():
        h_ref[...] = jnp.zeros_like(h_ref)

    x = x_ref[...]
    for j in range(n_blk):
        u = x[:, j * CH_BLOCK:(j + 1) * CH_BLOCK].astype(BF16)
        bu = _dot(u, wb_ref[j])
        for c in range(2 * nc):
            s_ref[c, block_rows(j), :] = bu[:, c * LANES:(c + 1) * LANES]
    a_re = are_ref[...]
    a_im = aim_ref[...]

    def body(t, carry):
        rows = pl.ds(t * STEP_PITCH, n_blk, stride=BLOCK_PITCH)
        b_re = jnp.concatenate([s_ref[c, rows, :] for c in range(nc)], axis=1)
        b_im = jnp.concatenate([s_ref[nc + c, rows, :] for c in range(nc)], axis=1)
        h_re, h_im = _complex_step(a_re, a_im, carry[0], carry[1], b_re, b_im)
        for c in range(nc):
            s_ref[c, rows, :] = h_re[:, c * LANES:(c + 1) * LANES]
            s_ref[nc + c, rows, :] = h_im[:, c * LANES:(c + 1) * LANES]
        return h_re, h_im

    h_re, h_im = lax.fori_loop(0, tb, body, (h_ref[:, :ns], h_ref[:, ns:]), unroll=8)
    h_ref[...] = jnp.concatenate([h_re, h_im], axis=1)
    hre_ref[...] = h_re
    him_ref[...] = h_im

    ys = []
    for j in range(n_blk):
        hj_re = jnp.concatenate([s_ref[c, block_rows(j), :] for c in range(nc)], axis=1)
        hj_im = jnp.concatenate([s_ref[nc + c, block_rows(j), :] for c in range(nc)], axis=1)
        ys.append(_dot(hj_re.astype(BF16), wcre_ref[j]) - _dot(hj_im.astype(BF16), wcim_ref[j]))
    y = jnp.concatenate(ys, axis=1) + d_ref[...] * x
    g_ref[...] = jax.nn.gelu(y).astype(g_ref.dtype)


def _scan_prompt(x, wb, wc_re, wc_im, a_re, a_im, d, *, tb):
    t, dm = x.shape
    n_blk, _, ns2 = wb.shape
    ns = ns2 // 2
    assert n_blk == SUBLANES and t % tb == 0
    assert (n_blk - 1) * BLOCK_PITCH < STEP_PITCH
    full = lambda shape: pl.BlockSpec(shape, lambda i: (0,) * len(shape))
    const = lambda shape: pl.BlockSpec(shape, lambda i: (0,) * len(shape), pipeline_mode=pl.Buffered(1))
    return pl.pallas_call(
        functools.partial(_scan_prompt_kernel, n_blk=n_blk, tb=tb, ns=ns),
        grid=(t // tb,),
        in_specs=[pl.BlockSpec((tb, dm), lambda i: (i, 0)), const(wb.shape), const(wc_re.shape),
                  const(wc_im.shape), full(a_re.shape), full(a_im.shape), full(d.shape)],
        out_specs=[pl.BlockSpec((tb, dm), lambda i: (i, 0)), full((n_blk, ns)), full((n_blk, ns))],
        out_shape=[jax.ShapeDtypeStruct((t, dm), BF16), jax.ShapeDtypeStruct((n_blk, ns), F32),
                   jax.ShapeDtypeStruct((n_blk, ns), F32)],
        scratch_shapes=[pltpu.VMEM((ns2 // LANES, tb * STEP_PITCH, LANES), F32), pltpu.VMEM((n_blk, ns2), F32)],
        compiler_params=_cparams("arbitrary"),
        name="s5_scan_prompt",
    )(x, wb, wc_re, wc_im, a_re, a_im, d)


def _scan_sample_kernel(x_ref, wb_ref, wcre_ref, wcim_ref, are_ref, aim_ref, d_ref, h0re_ref, h0im_ref,
                        g_ref, hre_ref, him_ref, s_ref, *, n_grp, n_step, ns):
    x = x_ref[...]
    s_ref[...] = _dot(x.astype(BF16), wb_ref[0])
    a_re = jnp.broadcast_to(are_ref[0], (SUBLANES, ns))
    a_im = jnp.broadcast_to(aim_ref[0], (SUBLANES, ns))

    def body(bg, _):
        b0 = pl.multiple_of(bg * SUBLANES, SUBLANES)
        h_re = h0re_ref[pl.ds(b0, SUBLANES), :]
        h_im = h0im_ref[pl.ds(b0, SUBLANES), :]
        for t in range(n_step):
            r0 = pl.multiple_of((bg * n_step + t) * SUBLANES, SUBLANES)
            tile = s_ref[pl.ds(r0, SUBLANES), :]
            h_re, h_im = _complex_step(a_re, a_im, h_re, h_im, tile[:, :ns], tile[:, ns:])
            s_ref[pl.ds(r0, SUBLANES), :] = jnp.concatenate([h_re, h_im], axis=1)
        hre_ref[pl.ds(b0, SUBLANES), :] = h_re
        him_ref[pl.ds(b0, SUBLANES), :] = h_im
        return 0

    lax.fori_loop(0, n_grp, body, 0)
    h = s_ref[...]
    y = _dot(h[:, :ns].astype(BF16), wcre_ref[0]) - _dot(h[:, ns:].astype(BF16), wcim_ref[0])
    g_ref[...] = jax.nn.gelu(y + d_ref[...] * x).astype(g_ref.dtype)


def _scan_sample(x, wb, wc_re, wc_im, a_re, a_im, d, h0_re, h0_im, *, n_step):
    rows, dm = x.shape
    n_blk, _, ns2 = wb.shape
    ns = ns2 // 2
    n_seq = h0_re.shape[0]
    assert rows == n_seq * n_step and n_seq % SUBLANES == 0
    col = lambda shape: pl.BlockSpec(shape, lambda j: (0, j))
    lead = lambda shape: pl.BlockSpec(shape, lambda j: (j,) + (0,) * (len(shape) - 1))
    return pl.pallas_call(
        functools.partial(_scan_sample_kernel, n_grp=n_seq // SUBLANES, n_step=n_step, ns=ns),
        grid=(n_blk,),
        in_specs=[col((rows, CH_BLOCK)), lead((1, CH_BLOCK, ns2)), lead((1, ns, CH_BLOCK)),
                  lead((1, ns, CH_BLOCK)), lead((1, 1, ns)), lead((1, 1, ns)), col((1, CH_BLOCK)),
                  col((n_seq, ns)), col((n_seq, ns))],
        out_specs=[col((rows, CH_BLOCK)), col((n_seq, ns)), col((n_seq, ns))],
        out_shape=[jax.ShapeDtypeStruct((rows, dm), BF16), jax.ShapeDtypeStruct((n_seq, n_blk * ns), F32),
                   jax.ShapeDtypeStruct((n_seq, n_blk * ns), F32)],
        scratch_shapes=[pltpu.VMEM((rows, ns2), F32)],
        compiler_params=_cparams("arbitrary"),
        name="s5_scan_sample",
    )(x, wb, wc_re, wc_im, a_re.reshape(n_blk, 1, ns), a_im.reshape(n_blk, 1, ns), d, h0_re, h0_im)


def _post_norm_store(o_ref, x, sub, g_ref, b_ref):
    o_ref[...] = _layer_norm_rows(DN_ALPHA * x + sub, g_ref[...], b_ref[...])


def _glu_kernel(a_ref, x_ref, w1_ref, w2_ref, g_ref, b_ref, o_ref):
    a = a_ref[...]
    h = a.astype(F32) * _sigmoid(_dot(a, w1_ref[...]))
    _post_norm_store(o_ref, x_ref[...], _dot(h.astype(BF16), w2_ref[...]), g_ref, b_ref)


def _glu(a, x, w1, w2, g, b, *, tm):
    t, dm = x.shape
    row = pl.BlockSpec((tm, dm), lambda i: (i, 0))
    const = lambda arr: pl.BlockSpec(arr.shape, lambda i: (0, 0), pipeline_mode=pl.Buffered(1))
    return pl.pallas_call(
        _glu_kernel, grid=(t // tm,),
        in_specs=[row, row, const(w1), const(w2), const(g), const(b)],
        out_specs=row, out_shape=jax.ShapeDtypeStruct((t, dm), F32),
        compiler_params=_cparams("parallel"), name="glu",
    )(a, x, w1, w2, g, b)


def _mlp_kernel(x_ref, w1_ref, w2_ref, g_ref, b_ref, o_ref, xb_ref):
    f = pl.program_id(1)

    @pl.when(f == 0)
    def _():
        xb_ref[...] = x_ref[...].astype(BF16)
        o_ref[...] = jnp.zeros_like(o_ref)

    h = jnp.square(jnp.maximum(_dot(xb_ref[...], w1_ref[...]), 0.0))
    o_ref[...] += _dot(h.astype(BF16), w2_ref[...])

    @pl.when(f == pl.num_programs(1) - 1)
    def _():
        for r0 in range(0, o_ref.shape[0], NORM_ROWS):
            rows = slice(r0, r0 + NORM_ROWS)
            o_ref[rows] = _layer_norm_rows(DN_ALPHA * x_ref[rows] + o_ref[rows], g_ref[...], b_ref[...])


def _mlp(x, w1, w2, layer, g, b, *, tm, tf):
    t, dm = x.shape
    once = pl.Buffered(1)
    row = pl.BlockSpec((tm, dm), lambda i, f: (i, 0), pipeline_mode=once)
    vec = pl.BlockSpec((1, dm), lambda i, f: (0, 0), pipeline_mode=once)
    w1s = pl.BlockSpec((None, dm, tf), lambda i, f: (layer, 0, f))
    w2s = pl.BlockSpec((None, tf, dm), lambda i, f: (layer, f, 0))
    return pl.pallas_call(
        _mlp_kernel, grid=(t // tm, w1.shape[2] // tf),
        in_specs=[row, w1s, w2s, vec, vec],
        out_specs=pl.BlockSpec((tm, dm), lambda i, f: (i, 0)),
        out_shape=jax.ShapeDtypeStruct((t, dm), F32),
        scratch_shapes=[pltpu.VMEM((tm, dm), BF16)],
        compiler_params=_cparams("parallel", "arbitrary"), name="mlp",
    )(x, w1, w2, g, b)


def _ple_kernel(x_ref, p_ref, wp_ref, wg_ref, g_ref, b_ref, o_ref):
    x = x_ref[...]
    sub = _dot(p_ref[...].astype(BF16), wp_ref[...]) * _sigmoid(_dot(x.astype(BF16), wg_ref[...]))
    _post_norm_store(o_ref, x, sub, g_ref, b_ref)


def _ple(x, p, wp, wg, layer, g, b, *, tm):
    t, dm = x.shape
    row = pl.BlockSpec((tm, dm), lambda i: (i, 0))
    full = lambda a: pl.BlockSpec(a.shape, lambda i: (0, 0))
    of_layer = lambda a: pl.BlockSpec((None,) + a.shape[1:], lambda i: (layer, 0, 0),
                                      pipeline_mode=pl.Buffered(1))
    return pl.pallas_call(
        _ple_kernel, grid=(t // tm,),
        in_specs=[row, pl.BlockSpec((tm, p.shape[1]), lambda i: (i, 0)), of_layer(wp), of_layer(wg),
                  full(g), full(b)],
        out_specs=row, out_shape=jax.ShapeDtypeStruct((t, dm), F32),
        compiler_params=_cparams("parallel"), name="ple",
    )(x, p, wp, wg, g, b)


def _attn_out_kernel(o_ref, x_ref, w_ref, g_ref, b_ref, y_ref):
    _post_norm_store(y_ref, x_ref[...], _dot(o_ref[...], w_ref[...]), g_ref, b_ref)


def _attn_out(o, x, w, g, b, *, tm):
    t, dm = x.shape
    row = pl.BlockSpec((tm, dm), lambda i: (i, 0))
    full = lambda a: pl.BlockSpec(a.shape, lambda i: (0, 0))
    return pl.pallas_call(
        _attn_out_kernel, grid=(t // tm,),
        in_specs=[row, row, full(w), full(g), full(b)],
        out_specs=row, out_shape=jax.ShapeDtypeStruct((t, dm), F32),
        compiler_params=_cparams("parallel"), name="attn_out",
    )(o, x, w, g, b)


def _dot_t(w_t, x):
    return lax.dot_general(w_t, x, (((1,), (1,)), ((), ())), preferred_element_type=F32)


def _qkv_prompt_kernel(x_ref, wqt_ref, wk_ref, wv_ref, wvt_ref, qt_ref, k_ref, v_ref, kb_ref, vt_ref):
    tm = x_ref.shape[0]
    n_kv = wk_ref.shape[1] // HEAD_DIM
    xb = x_ref[...].astype(BF16)
    qt_ref[...] = (_dot_t(wqt_ref[...], xb) * QUERY_SCALE).astype(qt_ref.dtype)
    k = _dot(xb, wk_ref[...])
    v = _dot(xb, wv_ref[...])
    kb_ref[...] = k.astype(kb_ref.dtype)
    vt_ref[0] = _dot_t(wvt_ref[...], xb).astype(vt_ref.dtype)
    for h in range(n_kv):
        k_ref[pl.ds(h, tm, stride=n_kv), :] = k[:, h * HEAD_DIM:(h + 1) * HEAD_DIM]
        v_ref[pl.ds(h, tm, stride=n_kv), :] = v[:, h * HEAD_DIM:(h + 1) * HEAD_DIM]


def _qkv_prompt(x, wqt, wk, wv, wvt, *, tm):
    t, dm = x.shape
    dq, dkv = wqt.shape[0], wk.shape[1]
    n_kv = dkv // HEAD_DIM
    row = lambda n: pl.BlockSpec((tm, n), lambda i: (i, 0))
    head_rows = pl.BlockSpec((tm * n_kv, HEAD_DIM), lambda i: (i, 0))
    const = lambda a: pl.BlockSpec(a.shape, lambda i: (0, 0), pipeline_mode=pl.Buffered(1))
    return pl.pallas_call(
        _qkv_prompt_kernel, grid=(t // tm,),
        in_specs=[row(dm), const(wqt), const(wk), const(wv), const(wvt)],
        out_specs=[pl.BlockSpec((dq, tm), lambda i: (0, i)), head_rows, head_rows, row(dkv),
                   pl.BlockSpec((1, dkv, tm), lambda i: (i, 0, 0))],
        out_shape=[jax.ShapeDtypeStruct((dq, t), BF16), jax.ShapeDtypeStruct((t * n_kv, HEAD_DIM), F32),
                   jax.ShapeDtypeStruct((t * n_kv, HEAD_DIM), F32), jax.ShapeDtypeStruct((t, dkv), BF16),
                   jax.ShapeDtypeStruct((t // tm, dkv, tm), BF16)],
        compiler_params=_cparams("parallel"), name="qkv_prompt",
    )(x, wqt, wk, wv, wvt)


def _qkv_sample_kernel(x_ref, wq_ref, wk_ref, wv_ref, q_ref, k_ref, v_ref):
    xb = x_ref[...].astype(BF16)
    q_ref[...] = (_dot(xb, wq_ref[...]) * QUERY_SCALE).astype(q_ref.dtype)
    k_ref[...] = _dot(xb, wk_ref[...])
    v_ref[...] = _dot(xb, wv_ref[...])


def _qkv_sample(x, wq, wk, wv, *, tm):
    t, dm = x.shape
    dq, dkv = wq.shape[1], wk.shape[1]
    row = lambda n: pl.BlockSpec((tm, n), lambda i: (i, 0))
    const = lambda a: pl.BlockSpec(a.shape, lambda i: (0, 0), pipeline_mode=pl.Buffered(1))
    return pl.pallas_call(
        _qkv_sample_kernel, grid=(t // tm,),
        in_specs=[row(dm), const(wq), const(wk), const(wv)],
        out_specs=[row(dq), row(dkv), row(dkv)],
        out_shape=[jax.ShapeDtypeStruct((t, dq), BF16), jax.ShapeDtypeStruct((t, dkv), F32),
                   jax.ShapeDtypeStruct((t, dkv), F32)],
        compiler_params=_cparams("parallel"), name="qkv_sample",
    )(x, wq, wk, wv)


MASKED_LOGIT = -1e30

STEPS_PER_TRIP = 4

EXP2_CLAMP = 126.0


def _softplus2(z):
    return jnp.maximum(jnp.log(1.0 + jnp.exp2(jnp.minimum(z, EXP2_CLAMP))) * LOG2E, z)


def _suffix_ones(n):
    return (lax.broadcasted_iota(jnp.int32, (n, n), 0) >= lax.broadcasted_iota(jnp.int32, (n, n), 1)).astype(BF16)


def _suffix_ones_t(n):
    return (lax.broadcasted_iota(jnp.int32, (n, n), 1) >= lax.broadcasted_iota(jnp.int32, (n, n), 0)).astype(BF16)


def _sb_scores(z, mask):
    if mask is not None:
        z = jnp.where(mask, z, MASKED_LOGIT)
    return z, _softplus2(z).astype(BF16)


def _sb_weights(z, c):
    return jnp.exp2(z - c).astype(BF16)


def _attn_prompt_kernel(bias_ref, qt_ref, k_ref, vt_ref, o_ref, acc_ref, carry_ref,
                        qa_ref, z0_ref, sp0_ref, z1_ref, sp1_ref, *, tq, q_per_kv, n_str):
    qi = pl.program_id(1)
    n = q_per_kv * tq
    u = _suffix_ones_t(tq)
    streams = range(n_str)
    head = lambda j: slice(j * HEAD_DIM, (j + 1) * HEAD_DIM)
    row = lax.broadcasted_iota(jnp.int32, (HEAD_DIM, n), 0)
    for j in streams:
        b2 = bias_ref[j] * LOG2E
        b_hi = b2.astype(BF16).astype(F32)
        extra = jnp.where(row == 0, b_hi, jnp.where(row == 1, b2 - b_hi, 0.0))
        qa_ref[j, :HEAD_DIM] = jnp.concatenate([qt_ref[head(j * q_per_kv + g), :] for g in range(q_per_kv)], axis=1)
        qa_ref[j, HEAD_DIM:] = extra.astype(BF16)
    ones2 = (lax.broadcasted_iota(jnp.int32, (tq, HEAD_DIM), 1) < 2).astype(BF16)
    acc_ref[...] = jnp.zeros_like(acc_ref)
    carry_ref[...] = jnp.zeros_like(carry_ref)
    stage = ((z0_ref, sp0_ref), (z1_ref, sp1_ref))

    def scores(kj, j):
        kb = k_ref[pl.ds(pl.multiple_of(kj * tq, tq), tq), head(j)]
        return _dot(jnp.concatenate([kb, ones2], axis=1), qa_ref[j])

    def suffix_sums(slot, j):
        return _dot(u, stage[slot][1][j]) + carry_ref[j]

    def weigh(kj, slot, j, c):
        acc_ref[j] += _dot(vt_ref[kj, head(j), :], _sb_weights(stage[slot][0][j], c))
        carry_ref[j] = c[:1, :]

    def score(kj, slot, mask):
        z_ref, sp_ref = stage[slot]
        for j in streams:
            z_ref[j], sp_ref[j] = _sb_scores(scores(kj, j), mask)

    def accumulate(kj, slot):
        cs = [suffix_sums(slot, j) for j in streams]
        for j in streams:
            weigh(kj, slot, j, cs[j])

    def step(kj, slot):
        zn_ref, spn_ref = stage[1 - slot]
        cs = [suffix_sums(slot, j) for j in streams]
        ss = [scores(kj - 1, j) for j in streams]
        for j in streams:
            zn_ref[j], spn_ref[j] = _sb_scores(ss[j], None)
        for j in streams:
            weigh(kj, slot, j, cs[j])

    key = lax.broadcasted_iota(jnp.int32, (tq, n), 0)
    query = lax.broadcasted_iota(jnp.int32, (tq, n), 1) % tq
    score(qi, 0, key < query)

    def steps(kj, count):
        for i in range(count):
            step(kj - i, i % 2)

    def unrolled(m, _):
        steps(qi - STEPS_PER_TRIP * m, STEPS_PER_TRIP)
        return 0

    lax.fori_loop(0, qi // STEPS_PER_TRIP, unrolled, 0)
    left = qi % STEPS_PER_TRIP
    for r in range(STEPS_PER_TRIP):
        @pl.when(left == r)
        def _(r=r):
            steps(r, r)
            accumulate(0, r % 2)

    o_ref[...] = jnp.concatenate([acc_ref[j, :, g * tq:(g + 1) * tq].T for j in streams for g in range(q_per_kv)],
                                 axis=1).astype(o_ref.dtype)


def _attn_prompt(bias_row, qt, kb, vt, *, tq, n_str):
    dq, t = qt.shape
    nb = t // tq
    n_kv = kb.shape[1] // HEAD_DIM
    q_per_kv = dq // HEAD_DIM // n_kv
    n = q_per_kv * tq
    assert vt.shape == (nb, n_kv * HEAD_DIM, tq) and bias_row.shape == (n_kv, 1, n) and n_kv % n_str == 0
    per_stream = lambda shape, dt: pltpu.VMEM((n_str,) + shape, dt)
    return pl.pallas_call(
        functools.partial(_attn_prompt_kernel, tq=tq, q_per_kv=q_per_kv, n_str=n_str),
        grid=(n_kv // n_str, nb),
        in_specs=[pl.BlockSpec((n_str, 1, n), lambda h, i: (h, 0, 0)),
                  pl.BlockSpec((n_str * q_per_kv * HEAD_DIM, tq), lambda h, i: (h, i)),
                  pl.BlockSpec((t, n_str * HEAD_DIM), lambda h, i: (0, h)),
                  pl.BlockSpec((nb, n_str * HEAD_DIM, tq), lambda h, i: (0, h, 0))],
        out_specs=pl.BlockSpec((tq, n_str * q_per_kv * HEAD_DIM), lambda h, i: (i, h)),
        out_shape=jax.ShapeDtypeStruct((t, dq), BF16),
        scratch_shapes=[per_stream((HEAD_DIM, n), F32), per_stream((1, n), F32), per_stream((2 * HEAD_DIM, n), BF16)]
        + [per_stream((tq, n), F32), per_stream((tq, n), BF16)] * 2,
        compiler_params=_cparams("parallel", "arbitrary"), name="attn_prompt",
    )(bias_row, qt, kb, vt)


def _attn_sample_kernel(pt_ref, q_ref, bias_ref, kn_ref, vn_ref, *refs, n_step, n_pages, n_sub):
    n_refs = n_sub * n_pages
    kc_refs, vc_refs, o_ref = refs[:n_refs], refs[n_refs:2 * n_refs], refs[2 * n_refs]
    rows = q_ref.shape[1]
    n_kv = kn_ref.shape[2] // HEAD_DIM
    page = kc_refs[0].shape[1] // n_kv
    rows_per_kv = rows // n_kv
    row_kv = lax.broadcasted_iota(jnp.int32, (rows, HEAD_DIM), 0) // rows_per_kv
    bias = bias_ref[...] * LOG2E

    def page_2d(ref):
        heads = [ref[0, pl.ds(h, page, stride=n_kv), :] for h in range(n_kv)]
        return jnp.concatenate(heads, axis=1).astype(BF16)

    def new_tokens(ref, s):
        pad = jnp.zeros((page - n_step, n_kv * HEAD_DIM), F32)
        return jnp.concatenate([ref[s], pad], axis=0).astype(BF16)

    def page_pair(page_refs, s, m):
        return jnp.concatenate([page_2d(page_refs[s * n_pages + 2 * m]),
                                page_2d(page_refs[s * n_pages + 2 * m + 1])], axis=0)

    def score(q_bd, k2, mask):
        s = lax.dot_general(q_bd, k2, (((1,), (1,)), ((), ())), preferred_element_type=F32)
        return _sb_scores(s + bias, mask)

    step = lax.broadcasted_iota(jnp.int32, (rows, page), 0) % n_step
    key = lax.broadcasted_iota(jnp.int32, (rows, page), 1)
    pairs = list(reversed(range(n_pages // 2)))
    u1, u2 = _suffix_ones(page), _suffix_ones(2 * page)
    scored = []
    for s in range(n_sub):
        q = q_ref[s]
        q_bd = jnp.concatenate([jnp.where(row_kv == h, q, jnp.zeros_like(q)) for h in range(n_kv)], axis=1)
        scored.append([score(q_bd, new_tokens(kn_ref, s), key < step)]
                      + [score(q_bd, page_pair(kc_refs, s, m), None) for m in pairs])
    sums = [[_dot(sp, u1 if i == 0 else u2) for i, (_, sp) in enumerate(blocks)] for blocks in scored]
    for s in range(n_sub):
        values = [new_tokens(vn_ref, s)] + [page_pair(vc_refs, s, m) for m in pairs]
        acc = jnp.zeros((rows, n_kv * HEAD_DIM), F32)
        carry = jnp.zeros((rows, 1), F32)
        for (z, _), c_local, v2 in zip(scored[s], sums[s], values):
            c = c_local + carry
            acc = acc + _dot(_sb_weights(z, c), v2)
            carry = c[:, :1]
        o = jnp.zeros((rows, HEAD_DIM), F32)
        for h in range(n_kv):
            o = o + jnp.where(row_kv == h, acc[:, h * HEAD_DIM:(h + 1) * HEAD_DIM], 0.0)
        o_ref[s] = o.astype(o_ref.dtype)


def _page_index(seq, n_sub, p, b, pt):
    return (pt[b * n_sub + seq, p], 0, 0)


def _attn_sample(page_table, q, bias_col, cache_k, cache_v, k_new, v_new, *, n_sub):
    n_seq, rows, _ = q.shape
    n_pages = page_table.shape[1]
    _, page, n_kv, _ = cache_k.shape
    n_step = k_new.shape[1]
    assert page % SUBLANES == 0 and n_step <= page and n_pages % 2 == 0 and n_seq % n_sub == 0
    per_seq = lambda shape: pl.BlockSpec((n_sub,) + shape, lambda b, pt: (b,) + (0,) * len(shape))
    cache_k = cache_k.reshape(-1, page * n_kv, HEAD_DIM)
    cache_v = cache_v.reshape(-1, page * n_kv, HEAD_DIM)
    pages = [pl.BlockSpec((1, page * n_kv, HEAD_DIM), functools.partial(_page_index, s, n_sub, p))
             for s in range(n_sub) for p in range(n_pages)]
    grid_spec = pltpu.PrefetchScalarGridSpec(
        num_scalar_prefetch=1, grid=(n_seq // n_sub,),
        in_specs=[per_seq((rows, HEAD_DIM)), pl.BlockSpec((rows, 1), lambda b, pt: (0, 0)),
                  per_seq((n_step, n_kv * HEAD_DIM)), per_seq((n_step, n_kv * HEAD_DIM))] + pages + pages,
        out_specs=per_seq((rows, HEAD_DIM)))
    return pl.pallas_call(
        functools.partial(_attn_sample_kernel, n_step=n_step, n_pages=n_pages, n_sub=n_sub),
        grid_spec=grid_spec, out_shape=jax.ShapeDtypeStruct((n_seq, rows, HEAD_DIM), BF16),
        compiler_params=_cparams("parallel"), name="attn_sample",
    )(page_table, q, bias_col, k_new, v_new, *([cache_k] * len(pages)), *([cache_v] * len(pages)))


def kernel(x_prompt, x_sample, p_prompt, p_sample, state_ssm_re, state_ssm_im, cache_k, cache_v, page_table, s5_lam_re, s5_lam_im, s5_log_dt, s5_b_re, s5_b_im, s5_c_re, s5_c_im, s5_d, s5_w_gate, s5_w_out, w_k, w_v, sb_w_q, sb_w_o, sb_bias, ln_mix_g, ln_mix_b, mlp_w1, mlp_w2, ln_mlp_g, ln_mlp_b, ple_w_proj, ple_w_gate, ln_ple_g, ln_ple_b):
    _, t_p, dm = x_prompt.shape
    n_seq, n_step, _ = x_sample.shape
    n_grp_total, p_state = s5_lam_re.shape[1:]
    n_blk = dm // CH_BLOCK
    gpb = CH_BLOCK // SSM_GROUP
    ns = gpb * p_state
    n_heads = sb_w_q.shape[2] // HEAD_DIM
    rows_s = n_seq * n_step
    assert x_prompt.shape[0] == 1 and n_step == SUBLANES and n_seq % SUBLANES == 0
    bf = lambda a: a.astype(BF16)
    vec = lambda a: a.reshape(1, -1)

    def to_group_order(a):
        c = a.shape[-1]
        return a.reshape(n_seq // SUBLANES, SUBLANES, n_step, c).transpose(0, 2, 1, 3).reshape(rows_s, c)

    def to_batch_order(a):
        c = a.shape[-1]
        return a.reshape(n_seq // SUBLANES, n_step, SUBLANES, c).transpose(0, 2, 1, 3).reshape(n_seq, n_step, c)

    xp = x_prompt.reshape(t_p, dm)
    xs = to_group_order(x_sample)
    pp = [p_prompt[i].reshape(t_p, -1) for i in range(DEPTH)]
    ps = [to_group_order(p_sample[i]) for i in range(DEPTH)]

    w_gate, w_out, w_o = bf(s5_w_gate[0]), bf(s5_w_out[0]), bf(sb_w_o[0])
    w1, w2, wp, wg = bf(mlp_w1), bf(mlp_w2), bf(ple_w_proj), bf(ple_w_gate)
    wq, wk, wv = bf(sb_w_q[0]), bf(w_k), bf(w_v)

    def glu(g, x):
        return _glu(g, x, w_gate, w_out, vec(ln_mix_g[0]), vec(ln_mix_b[0]), tm=256)

    def attn_out(o, x):
        return _attn_out(o, x, w_o, vec(ln_mix_g[1]), vec(ln_mix_b[1]), tm=512)

    def mlp(x, i):
        return _mlp(x, w1, w2, i, vec(ln_mlp_g[i]), vec(ln_mlp_b[i]), tm=1024, tf=1024)

    def ple(x, p, i):
        return _ple(x, p, wp, wg, i, vec(ln_ple_g[i]), vec(ln_ple_b[i]), tm=512)

    ab_re, ab_im, bb_re, bb_im = _s5_prep(s5_lam_re[0], s5_lam_im[0], s5_log_dt[0], s5_b_re[0], s5_b_im[0])
    wb = bf(jnp.concatenate([_block_diag_in(bb_re, n_blk, gpb, p_state),
                             _block_diag_in(bb_im, n_blk, gpb, p_state)], axis=-1))
    wc_re = bf(_block_diag_out(s5_c_re[0], n_blk, gpb, p_state))
    wc_im = bf(_block_diag_out(s5_c_im[0], n_blk, gpb, p_state))
    a_re = ab_re.reshape(n_blk, ns)
    a_im = ab_im.reshape(n_blk, ns)
    d = vec(s5_d[0])
    g_p, hp_re, hp_im = _scan_prompt(xp, wb, wc_re, wc_im, a_re, a_im, d, tb=128)
    g_s, hs_re, hs_im = _scan_sample(xs, wb, wc_re, wc_im, a_re, a_im, d,
                                     state_ssm_re[0].reshape(n_seq, -1), state_ssm_im[0].reshape(n_seq, -1),
                                     n_step=n_step)
    xp = ple(mlp(glu(g_p, xp), 0), pp[0], 0)
    xs = ple(mlp(glu(g_s, xs), 0), ps[0], 0)

    tq = 256
    q_per_kv = n_heads // N_KV_HEADS
    qt, k_p, v_p, kb, vt = _qkv_prompt(xp, wq.T, wk, wv, wv.T, tm=tq)
    bias_row = jnp.repeat(sb_bias[0].reshape(N_KV_HEADS, q_per_kv), tq, axis=1)
    o_p = _attn_prompt(bias_row.reshape(N_KV_HEADS, 1, q_per_kv * tq), qt, kb, vt, tq=tq, n_str=2)
    q_s, k_s, v_s = _qkv_sample(xs, wq, wk, wv, tm=256)
    k_s = to_batch_order(k_s)
    v_s = to_batch_order(v_s)
    q_s = to_batch_order(q_s).reshape(n_seq, n_step, n_heads, HEAD_DIM).transpose(0, 2, 1, 3)
    bias_col = jnp.repeat(sb_bias[0], n_step).reshape(n_heads * n_step, 1)
    o_s = _attn_sample(page_table, q_s.reshape(n_seq, n_heads * n_step, HEAD_DIM), bias_col,
                       cache_k, cache_v, k_s, v_s, n_sub=1)
    o_s = o_s.reshape(n_seq, n_heads, n_step, HEAD_DIM).transpose(0, 2, 1, 3).reshape(n_seq, n_step, -1)
    xp = ple(mlp(attn_out(o_p, xp), 1), pp[1], 1)
    xs = ple(mlp(attn_out(to_group_order(o_s), xs), 1), ps[1], 1)

    state = lambda a: a.reshape(1, -1, n_grp_total, p_state)
    kv = lambda a, lead: a.reshape(lead + (N_KV_HEADS, HEAD_DIM))
    return (xp.reshape(1, t_p, dm), to_batch_order(xs),
            state(hp_re), state(hp_im), kv(k_p, (1, t_p)), kv(v_p, (1, t_p)),
            state(hs_re), state(hs_im), kv(k_s, (n_seq, n_step)), kv(v_s, (n_seq, n_step)))
```

```python
import functools
import math

import jax
import jax.numpy as jnp
from jax import lax
from jax.experimental import pallas as pl
from jax.experimental.pallas import tpu as pltpu

F32 = jnp.float32
BF16 = jnp.bfloat16

SSM_GROUP = 16
HEAD_DIM = 128
N_KV_HEADS = 4
LN_EPS = 1e-5
DEPTH = 2
DN_ALPHA = (2.0 * DEPTH) ** 0.25
LOG2E = 1.4426950408889634
QUERY_SCALE = HEAD_DIM ** -0.5 * LOG2E

SUBLANES = 8
LANES = 128
CH_BLOCK = 256
STEP_PITCH = 17
BLOCK_PITCH = 2
VMEM_LIMIT = 56 * 1024 * 1024
NORM_ROWS = 256


def _cparams(*sem):
    return pltpu.CompilerParams(dimension_semantics=sem, vmem_limit_bytes=VMEM_LIMIT)


def _dot(a, b):
    return jnp.dot(a, b, preferred_element_type=F32)


def _sigmoid(x):
    return 1.0 / (1.0 + jnp.exp(-x))


def _layer_norm_rows(y, g, b):
    mu = jnp.mean(y, axis=-1, keepdims=True)
    yc = y - mu
    var = jnp.mean(yc * yc, axis=-1, keepdims=True)
    return yc * lax.rsqrt(var + LN_EPS) * g + b


def _discretize(lam_re, lam_im, dt):
    mag = jnp.exp(lam_re * dt)
    ang = lam_im * dt
    ab_re = mag * jnp.cos(ang)
    ab_im = mag * jnp.sin(ang)
    den = lam_re * lam_re + lam_im * lam_im
    nr = ab_re - 1.0
    f_re = (nr * lam_re + ab_im * lam_im) / den
    f_im = (ab_im * lam_re - nr * lam_im) / den
    return ab_re, ab_im, f_re, f_im


def _s5_prep_kernel(lre_ref, lim_ref, ldt_ref, lre_rep_ref, lim_rep_ref, bre_ref, bim_ref,
                    abre_ref, abim_ref, bbre_ref, bbim_ref):
    dt = jnp.exp(ldt_ref[...])
    ab_re, ab_im, _, _ = _discretize(lre_ref[...], lim_ref[...], dt)
    abre_ref[...] = ab_re
    abim_ref[...] = ab_im
    _, _, f_re, f_im = _discretize(lre_rep_ref[...], lim_rep_ref[...], dt)
    b_re = bre_ref[...]
    b_im = bim_ref[...]
    bbre_ref[...] = f_re * b_re - f_im * b_im
    bbim_ref[...] = f_re * b_im + f_im * b_re


def _s5_prep(lam_re, lam_im, log_dt, b_re, b_im):
    g, p = lam_re.shape
    rep = lambda a: jnp.repeat(a, SSM_GROUP, axis=1)
    out = [jax.ShapeDtypeStruct((g, p), F32)] * 2 + [jax.ShapeDtypeStruct((g, p * SSM_GROUP), F32)] * 2
    return pl.pallas_call(_s5_prep_kernel, out_shape=out, name="s5_prep")(
        lam_re, lam_im, log_dt.reshape(g, 1), rep(lam_re), rep(lam_im),
        b_re.reshape(g, p * SSM_GROUP), b_im.reshape(g, p * SSM_GROUP))


def _block_diag(v, width):
    n_blk, rows, _ = v.shape
    gpb = CH_BLOCK // SSM_GROUP
    tiled = jnp.tile(v, (1, 1, gpb))
    row_grp = lax.broadcasted_iota(jnp.int32, (rows, gpb * width), 0) // (rows // gpb)
    col_grp = lax.broadcasted_iota(jnp.int32, (rows, gpb * width), 1) // width
    return jnp.where((row_grp == col_grp)[None], tiled, 0.0).astype(BF16)


def _block_diag_in(bb, n_blk, gpb, p):
    v = bb.reshape(n_blk, gpb, p, SSM_GROUP).transpose(0, 1, 3, 2)
    return _block_diag(v.reshape(n_blk, gpb * SSM_GROUP, p), p)


def _block_diag_out(c, n_blk, gpb, p):
    v = c.reshape(n_blk, gpb, SSM_GROUP, p).transpose(0, 1, 3, 2)
    return _block_diag(v.reshape(n_blk, gpb * p, SSM_GROUP), SSM_GROUP)


def _complex_step(a_re, a_im, h_re, h_im, b_re, b_im):
    return (a_re * h_re - a_im * h_im + b_re, a_re * h_im + a_im * h_re + b_im)


def _scan_prompt_kernel(x_ref, wb_ref, wcre_ref, wcim_ref, are_ref, aim_ref, d_ref,
                        g_ref, hre_ref, him_ref, s_ref, h_ref, *, n_blk, tb, ns):
    nc = ns // LANES
    block_rows = lambda j: pl.ds(j * BLOCK_PITCH, tb, stride=STEP_PITCH)

    @pl.when(pl.program_id(0) == 0)
    def _():
        h_ref[...] = jnp.zeros_like(h_ref)

    x = x_ref[...]
    for j in range(n_blk):
        u = x[:, j * CH_BLOCK:(j + 1) * CH_BLOCK].astype(BF16)
        bu = _dot(u, wb_ref[j])
        for c in range(2 * nc):
            s_ref[c, block_rows(j), :] = bu[:, c * LANES:(c + 1) * LANES]
    a_re = are_ref[...]
    a_im = aim_ref[...]

    def body(t, carry):
        rows = pl.ds(t * STEP_PITCH, n_blk, stride=BLOCK_PITCH)
        b_re = jnp.concatenate([s_ref[c, rows, :] for c in range(nc)], axis=1)
        b_im = jnp.concatenate([s_ref[nc + c, rows, :] for c in range(nc)], axis=1)
        h_re, h_im = _complex_step(a_re, a_im, carry[0], carry[1], b_re, b_im)
        for c in range(nc):
            s_ref[c, rows, :] = h_re[:, c * LANES:(c + 1) * LANES]
            s_ref[nc + c, rows, :] = h_im[:, c * LANES:(c + 1) * LANES]
        return h_re, h_im

    h_re, h_im = lax.fori_loop(0, tb, body, (h_ref[:, :ns], h_ref[:, ns:]), unroll=8)
    h_ref[...] = jnp.concatenate([h_re, h_im], axis=1)
    hre_ref[...] = h_re
    him_ref[...] = h_im

    ys = []
    for j in range(n_blk):
        hj_re = jnp.concatenate([s_ref[c, block_rows(j), :] for c in range(nc)], axis=1)
        hj_im = jnp.concatenate([s_ref[nc + c, block_rows(j), :] for c in range(nc)], axis=1)
        ys.append(_dot(hj_re.astype(BF16), wcre_ref[j]) - _dot(hj_im.astype(BF16), wcim_ref[j]))
    y = jnp.concatenate(ys, axis=1) + d_ref[...] * x
    g_ref[...] = jax.nn.gelu(y).astype(g_ref.dtype)


def _scan_prompt(x, wb, wc_re, wc_im, a_re, a_im, d, *, tb):
    t, dm = x.shape
    n_blk, _, ns2 = wb.shape
    ns = ns2 // 2
    assert n_blk == SUBLANES and t % tb == 0
    assert (n_blk - 1) * BLOCK_PITCH < STEP_PITCH
    full = lambda shape: pl.BlockSpec(shape, lambda i: (0,) * len(shape))
    const = lambda shape: pl.BlockSpec(shape, lambda i: (0,) * len(shape), pipeline_mode=pl.Buffered(1))
    return pl.pallas_call(
        functools.partial(_scan_prompt_kernel, n_blk=n_blk, tb=tb, ns=ns),
        grid=(t // tb,),
        in_specs=[pl.BlockSpec((tb, dm), lambda i: (i, 0)), const(wb.shape), const(wc_re.shape),
                  const(wc_im.shape), full(a_re.shape), full(a_im.shape), full(d.shape)],
        out_specs=[pl.BlockSpec((tb, dm), lambda i: (i, 0)), full((n_blk, ns)), full((n_blk, ns))],
        out_shape=[jax.ShapeDtypeStruct((t, dm), BF16), jax.ShapeDtypeStruct((n_blk, ns), F32),
                   jax.ShapeDtypeStruct((n_blk, ns), F32)],
        scratch_shapes=[pltpu.VMEM((ns2 // LANES, tb * STEP_PITCH, LANES), F32), pltpu.VMEM((n_blk, ns2), F32)],
        compiler_params=_cparams("arbitrary"),
        name="s5_scan_prompt",
    )(x, wb, wc_re, wc_im, a_re, a_im, d)


def _scan_sample_kernel(x_ref, wb_ref, wcre_ref, wcim_ref, are_ref, aim_ref, d_ref, h0re_ref, h0im_ref,
                        g_ref, hre_ref, him_ref, s_ref, *, n_grp, n_step, ns):
    x = x_ref[...]
    s_ref[...] = _dot(x.astype(BF16), wb_ref[0])
    a_re = jnp.broadcast_to(are_ref[0], (SUBLANES, ns))
    a_im = jnp.broadcast_to(aim_ref[0], (SUBLANES, ns))

    def body(bg, _):
        b0 = pl.multiple_of(bg * SUBLANES, SUBLANES)
        h_re = h0re_ref[pl.ds(b0, SUBLANES), :]
        h_im = h0im_ref[pl.ds(b0, SUBLANES), :]
        for t in range(n_step):
            r0 = pl.multiple_of((bg * n_step + t) * SUBLANES, SUBLANES)
            tile = s_ref[pl.ds(r0, SUBLANES), :]
            h_re, h_im = _complex_step(a_re, a_im, h_re, h_im, tile[:, :ns], tile[:, ns:])
            s_ref[pl.ds(r0, SUBLANES), :] = jnp.concatenate([h_re, h_im], axis=1)
        hre_ref[pl.ds(b0, SUBLANES), :] = h_re
        him_ref[pl.ds(b0, SUBLANES), :] = h_im
        return 0

    lax.fori_loop(0, n_grp, body, 0)
    h = s_ref[...]
    y = _dot(h[:, :ns].astype(BF16), wcre_ref[0]) - _dot(h[:, ns:].astype(BF16), wcim_ref[0])
    g_ref[...] = jax.nn.gelu(y + d_ref[...] * x).astype(g_ref.dtype)


def _scan_sample(x, wb, wc_re, wc_im, a_re, a_im, d, h0_re, h0_im, *, n_step):
    rows, dm = x.shape
    n_blk, _, ns2 = wb.shape
    ns = ns2 // 2
    n_seq = h0_re.shape[0]
    assert rows == n_seq * n_step and n_seq % SUBLANES == 0
    col = lambda shape: pl.BlockSpec(shape, lambda j: (0, j))
    lead = lambda shape: pl.BlockSpec(shape, lambda j: (j,) + (0,) * (len(shape) - 1))
    return pl.pallas_call(
        functools.partial(_scan_sample_kernel, n_grp=n_seq // SUBLANES, n_step=n_step, ns=ns),
        grid=(n_blk,),
        in_specs=[col((rows, CH_BLOCK)), lead((1, CH_BLOCK, ns2)), lead((1, ns, CH_BLOCK)),
                  lead((1, ns, CH_BLOCK)), lead((1, 1, ns)), lead((1, 1, ns)), col((1, CH_BLOCK)),
                  col((n_seq, ns)), col((n_seq, ns))],
        out_specs=[col((rows, CH_BLOCK)), col((n_seq, ns)), col((n_seq, ns))],
        out_shape=[jax.ShapeDtypeStruct((rows, dm), BF16), jax.ShapeDtypeStruct((n_seq, n_blk * ns), F32),
                   jax.ShapeDtypeStruct((n_seq, n_blk * ns), F32)],
        scratch_shapes=[pltpu.VMEM((rows, ns2), F32)],
        compiler_params=_cparams("arbitrary"),
        name="s5_scan_sample",
    )(x, wb, wc_re, wc_im, a_re.reshape(n_blk, 1, ns), a_im.reshape(n_blk, 1, ns), d, h0_re, h0_im)


def _post_norm_store(o_ref, x, sub, g_ref, b_ref):
    o_ref[...] = _layer_norm_rows(DN_ALPHA * x + sub, g_ref[...], b_ref[...])


def _glu_kernel(a_ref, x_ref, w1_ref, w2_ref, g_ref, b_ref, o_ref):
    a = a_ref[...]
    h = a.astype(F32) * _sigmoid(_dot(a, w1_ref[...]))
    _post_norm_store(o_ref, x_ref[...], _dot(h.astype(BF16), w2_ref[...]), g_ref, b_ref)


def _glu(a, x, w1, w2, g, b, *, tm):
    t, dm = x.shape
    row = pl.BlockSpec((tm, dm), lambda i: (i, 0))
    const = lambda arr: pl.BlockSpec(arr.shape, lambda i: (0, 0), pipeline_mode=pl.Buffered(1))
    return pl.pallas_call(
        _glu_kernel, grid=(t // tm,),
        in_specs=[row, row, const(w1), const(w2), const(g), const(b)],
        out_specs=row, out_shape=jax.ShapeDtypeStruct((t, dm), F32),
        compiler_params=_cparams("parallel"), name="glu",
    )(a, x, w1, w2, g, b)


def _mlp_kernel(x_ref, w1_ref, w2_ref, g_ref, b_ref, o_ref, xb_ref):
    f = pl.program_id(1)

    @pl.when(f == 0)
    def _():
        xb_ref[...] = x_ref[...].astype(BF16)
        o_ref[...] = jnp.zeros_like(o_ref)

    h = jnp.square(jnp.maximum(_dot(xb_ref[...], w1_ref[...].astype(BF16)), 0.0))
    o_ref[...] += _dot(h.astype(BF16), w2_ref[...].astype(BF16))

    @pl.when(f == pl.num_programs(1) - 1)
    def _():
        for r0 in range(0, o_ref.shape[0], NORM_ROWS):
            rows = slice(r0, r0 + NORM_ROWS)
            o_ref[rows] = _layer_norm_rows(DN_ALPHA * x_ref[rows] + o_ref[rows], g_ref[...], b_ref[...])


def _mlp(x, w1, w2, layer, g, b, *, tm, tf):
    t, dm = x.shape
    once = pl.Buffered(1)
    row = pl.BlockSpec((tm, dm), lambda i, f: (i, 0), pipeline_mode=once)
    vec = pl.BlockSpec((1, dm), lambda i, f: (0, 0), pipeline_mode=once)
    w1s = pl.BlockSpec((None, dm, tf), lambda i, f: (layer, 0, f))
    w2s = pl.BlockSpec((None, tf, dm), lambda i, f: (layer, f, 0))
    return pl.pallas_call(
        _mlp_kernel, grid=(t // tm, w1.shape[2] // tf),
        in_specs=[row, w1s, w2s, vec, vec],
        out_specs=pl.BlockSpec((tm, dm), lambda i, f: (i, 0)),
        out_shape=jax.ShapeDtypeStruct((t, dm), F32),
        scratch_shapes=[pltpu.VMEM((tm, dm), BF16)],
        compiler_params=_cparams("parallel", "arbitrary"), name="mlp",
    )(x, w1, w2, g, b)


def _ple_kernel(x_ref, p_ref, wp_ref, wg_ref, g_ref, b_ref, o_ref):
    x = x_ref[...]
    sub = _dot(p_ref[...].astype(BF16), wp_ref[...]) * _sigmoid(_dot(x.astype(BF16), wg_ref[...]))
    _post_norm_store(o_ref, x, sub, g_ref, b_ref)


def _ple(x, p, wp, wg, layer, g, b, *, tm):
    t, dm = x.shape
    row = pl.BlockSpec((tm, dm), lambda i: (i, 0))
    full = lambda a: pl.BlockSpec(a.shape, lambda i: (0, 0))
    of_layer = lambda a: pl.BlockSpec((None,) + a.shape[1:], lambda i: (layer, 0, 0),
                                      pipeline_mode=pl.Buffered(1))
    return pl.pallas_call(
        _ple_kernel, grid=(t // tm,),
        in_specs=[row, pl.BlockSpec((tm, p.shape[1]), lambda i: (i, 0)), of_layer(wp), of_layer(wg),
                  full(g), full(b)],
        out_specs=row, out_shape=jax.ShapeDtypeStruct((t, dm), F32),
        compiler_params=_cparams("parallel"), name="ple",
    )(x, p, wp, wg, g, b)


def _attn_out_kernel(o_ref, x_ref, w_ref, g_ref, b_ref, y_ref):
    _post_norm_store(y_ref, x_ref[...], _dot(o_ref[...], w_ref[...]), g_ref, b_ref)


def _attn_out(o, x, w, g, b, *, tm):
    t, dm = x.shape
    row = pl.BlockSpec((tm, dm), lambda i: (i, 0))
    full = lambda a: pl.BlockSpec(a.shape, lambda i: (0, 0))
    return pl.pallas_call(
        _attn_out_kernel, grid=(t // tm,),
        in_specs=[row, row, full(w), full(g), full(b)],
        out_specs=row, out_shape=jax.ShapeDtypeStruct((t, dm), F32),
        compiler_params=_cparams("parallel"), name="attn_out",
    )(o, x, w, g, b)


def _dot_t(w_t, x):
    return lax.dot_general(w_t, x, (((1,), (1,)), ((), ())), preferred_element_type=F32)


def _qkv_prompt_kernel(x_ref, wqt_ref, wk_ref, wv_ref, wvt_ref, qt_ref, k_ref, v_ref, kb_ref, vt_ref):
    tm = x_ref.shape[0]
    n_kv = wk_ref.shape[1] // HEAD_DIM
    xb = x_ref[...].astype(BF16)
    qt_ref[...] = (_dot_t(wqt_ref[...], xb) * QUERY_SCALE).astype(qt_ref.dtype)
    k = _dot(xb, wk_ref[...])
    v = _dot(xb, wv_ref[...])
    kb_ref[...] = k.astype(kb_ref.dtype)
    vt_ref[0] = _dot_t(wvt_ref[...], xb).astype(vt_ref.dtype)
    for h in range(n_kv):
        k_ref[pl.ds(h, tm, stride=n_kv), :] = k[:, h * HEAD_DIM:(h + 1) * HEAD_DIM]
        v_ref[pl.ds(h, tm, stride=n_kv), :] = v[:, h * HEAD_DIM:(h + 1) * HEAD_DIM]


def _qkv_prompt(x, wqt, wk, wv, wvt, *, tm):
    t, dm = x.shape
    dq, dkv = wqt.shape[0], wk.shape[1]
    n_kv = dkv // HEAD_DIM
    row = lambda n: pl.BlockSpec((tm, n), lambda i: (i, 0))
    head_rows = pl.BlockSpec((tm * n_kv, HEAD_DIM), lambda i: (i, 0))
    const = lambda a: pl.BlockSpec(a.shape, lambda i: (0, 0), pipeline_mode=pl.Buffered(1))
    return pl.pallas_call(
        _qkv_prompt_kernel, grid=(t // tm,),
        in_specs=[row(dm), const(wqt), const(wk), const(wv), const(wvt)],
        out_specs=[pl.BlockSpec((dq, tm), lambda i: (0, i)), head_rows, head_rows, row(dkv),
                   pl.BlockSpec((1, dkv, tm), lambda i: (i, 0, 0))],
        out_shape=[jax.ShapeDtypeStruct((dq, t), BF16), jax.ShapeDtypeStruct((t * n_kv, HEAD_DIM), F32),
                   jax.ShapeDtypeStruct((t * n_kv, HEAD_DIM), F32), jax.ShapeDtypeStruct((t, dkv), BF16),
                   jax.ShapeDtypeStruct((t // tm, dkv, tm), BF16)],
        compiler_params=_cparams("parallel"), name="qkv_prompt",
    )(x, wqt, wk, wv, wvt)


def _qkv_sample_kernel(x_ref, wq_ref, wk_ref, wv_ref, q_ref, k_ref, v_ref):
    xb = x_ref[...].astype(BF16)
    q_ref[...] = (_dot(xb, wq_ref[...]) * QUERY_SCALE).astype(q_ref.dtype)
    k_ref[...] = _dot(xb, wk_ref[...])
    v_ref[...] = _dot(xb, wv_ref[...])


def _qkv_sample(x, wq, wk, wv, *, tm):
    t, dm = x.shape
    dq, dkv = wq.shape[1], wk.shape[1]
    row = lambda n: pl.BlockSpec((tm, n), lambda i: (i, 0))
    const = lambda a: pl.BlockSpec(a.shape, lambda i: (0, 0), pipeline_mode=pl.Buffered(1))
    return pl.pallas_call(
        _qkv_sample_kernel, grid=(t // tm,),
        in_specs=[row(dm), const(wq), const(wk), const(wv)],
        out_specs=[row(dq), row(dkv), row(dkv)],
        out_shape=[jax.ShapeDtypeStruct((t, dq), BF16), jax.ShapeDtypeStruct((t, dkv), F32),
                   jax.ShapeDtypeStruct((t, dkv), F32)],
        compiler_params=_cparams("parallel"), name="qkv_sample",
    )(x, wq, wk, wv)


MASKED_LOGIT = -1e30

STEPS_PER_TRIP = 4

EXP2_CLAMP = 126.0


def _softplus2(z):
    return jnp.maximum(jnp.log(1.0 + jnp.exp2(jnp.minimum(z, EXP2_CLAMP))) * LOG2E, z)


def _suffix_ones(n):
    return (lax.broadcasted_iota(jnp.int32, (n, n), 0) > lax.broadcasted_iota(jnp.int32, (n, n), 1)).astype(BF16)


def _suffix_ones_t(n):
    return (lax.broadcasted_iota(jnp.int32, (n, n), 1) > lax.broadcasted_iota(jnp.int32, (n, n), 0)).astype(BF16)


def _sb_scores(z, mask):
    if mask is not None:
        z = jnp.where(mask, z, MASKED_LOGIT)
    sp = _softplus2(z)
    return z - sp, sp.astype(BF16)


def _sb_weights(log_beta, c):
    return jnp.exp2(log_beta - c).astype(BF16)


def _attn_prompt_kernel(bias_ref, qt_ref, k_ref, vt_ref, o_ref, acc_ref, carry_ref,
                        qa_ref, z0_ref, sp0_ref, z1_ref, sp1_ref, *, tq, q_per_kv, n_str):
    qi = pl.program_id(1)
    n = q_per_kv * tq
    u = _suffix_ones_t(tq)
    streams = range(n_str)
    head = lambda j: slice(j * HEAD_DIM, (j + 1) * HEAD_DIM)
    row = lax.broadcasted_iota(jnp.int32, (HEAD_DIM, n), 0)
    for j in streams:
        b2 = bias_ref[j] * LOG2E
        b_hi = b2.astype(BF16).astype(F32)
        extra = jnp.where(row == 0, b_hi, jnp.where(row == 1, b2 - b_hi, 0.0))
        qa_ref[j, :HEAD_DIM] = jnp.concatenate([qt_ref[head(j * q_per_kv + g), :] for g in range(q_per_kv)], axis=1)
        qa_ref[j, HEAD_DIM:] = extra.astype(BF16)
    ones2 = (lax.broadcasted_iota(jnp.int32, (tq, HEAD_DIM), 1) < 2).astype(BF16)
    acc_ref[...] = jnp.zeros_like(acc_ref)
    carry_ref[...] = jnp.zeros_like(carry_ref)
    stage = ((z0_ref, sp0_ref), (z1_ref, sp1_ref))

    def scores(kj, j):
        kb = k_ref[pl.ds(pl.multiple_of(kj * tq, tq), tq), head(j)]
        return _dot(jnp.concatenate([kb, ones2], axis=1), qa_ref[j])

    def suffix_sums(slot, j):
        return _dot(u, stage[slot][1][j]) + carry_ref[j]

    def weigh(kj, slot, j, c):
        acc_ref[j] += _dot(vt_ref[kj, head(j), :], _sb_weights(stage[slot][0][j], c))
        carry_ref[j] = c[:1, :] + stage[slot][1][j, :1, :].astype(F32)

    def score(kj, slot, mask):
        z_ref, sp_ref = stage[slot]
        for j in streams:
            z_ref[j], sp_ref[j] = _sb_scores(scores(kj, j), mask)

    def accumulate(kj, slot):
        cs = [suffix_sums(slot, j) for j in streams]
        for j in streams:
            weigh(kj, slot, j, cs[j])

    def step(kj, slot):
        zn_ref, spn_ref = stage[1 - slot]
        cs = [suffix_sums(slot, j) for j in streams]
        ss = [scores(kj - 1, j) for j in streams]
        for j in streams:
            zn_ref[j], spn_ref[j] = _sb_scores(ss[j], None)
        for j in streams:
            weigh(kj, slot, j, cs[j])

    key = lax.broadcasted_iota(jnp.int32, (tq, n), 0)
    query = lax.broadcasted_iota(jnp.int32, (tq, n), 1) % tq
    score(qi, 0, key < query)

    def steps(kj, count):
        for i in range(count):
            step(kj - i, i % 2)

    def unrolled(m, _):
        steps(qi - STEPS_PER_TRIP * m, STEPS_PER_TRIP)
        return 0

    lax.fori_loop(0, qi // STEPS_PER_TRIP, unrolled, 0)
    left = qi % STEPS_PER_TRIP
    for r in range(STEPS_PER_TRIP):
        @pl.when(left == r)
        def _(r=r):
            steps(r, r)
            accumulate(0, r % 2)

    o_ref[...] = jnp.concatenate([acc_ref[j, :, g * tq:(g + 1) * tq].T for j in streams for g in range(q_per_kv)],
                                 axis=1).astype(o_ref.dtype)


def _attn_prompt(bias_row, qt, kb, vt, *, tq, n_str):
    dq, t = qt.shape
    nb = t // tq
    n_kv = kb.shape[1] // HEAD_DIM
    q_per_kv = dq // HEAD_DIM // n_kv
    n = q_per_kv * tq
    assert vt.shape == (nb, n_kv * HEAD_DIM, tq) and bias_row.shape == (n_kv, 1, n) and n_kv % n_str == 0
    per_stream = lambda shape, dt: pltpu.VMEM((n_str,) + shape, dt)
    return pl.pallas_call(
        functools.partial(_attn_prompt_kernel, tq=tq, q_per_kv=q_per_kv, n_str=n_str),
        grid=(n_kv // n_str, nb),
        in_specs=[pl.BlockSpec((n_str, 1, n), lambda h, i: (h, 0, 0)),
                  pl.BlockSpec((n_str * q_per_kv * HEAD_DIM, tq), lambda h, i: (h, i)),
                  pl.BlockSpec((t, n_str * HEAD_DIM), lambda h, i: (0, h)),
                  pl.BlockSpec((nb, n_str * HEAD_DIM, tq), lambda h, i: (0, h, 0))],
        out_specs=pl.BlockSpec((tq, n_str * q_per_kv * HEAD_DIM), lambda h, i: (i, h)),
        out_shape=jax.ShapeDtypeStruct((t, dq), BF16),
        scratch_shapes=[per_stream((HEAD_DIM, n), F32), per_stream((1, n), F32), per_stream((2 * HEAD_DIM, n), BF16)]
        + [per_stream((tq, n), F32), per_stream((tq, n), BF16)] * 2,
        compiler_params=_cparams("parallel", "arbitrary"), name="attn_prompt",
    )(bias_row, qt, kb, vt)


def _attn_sample_kernel(pt_ref, q_ref, bias_ref, kn_ref, vn_ref, *refs, n_step, n_pages, n_sub):
    n_refs = n_sub * n_pages
    kc_refs, vc_refs, o_ref = refs[:n_refs], refs[n_refs:2 * n_refs], refs[2 * n_refs]
    rows = q_ref.shape[1]
    n_kv = kn_ref.shape[2] // HEAD_DIM
    page = kc_refs[0].shape[1] // n_kv
    rows_per_kv = rows // n_kv
    row_kv = lax.broadcasted_iota(jnp.int32, (rows, HEAD_DIM), 0) // rows_per_kv
    bias = bias_ref[...] * LOG2E

    def page_2d(ref):
        heads = [ref[0, pl.ds(h, page, stride=n_kv), :] for h in range(n_kv)]
        return jnp.concatenate(heads, axis=1).astype(BF16)

    def new_tokens(ref, s):
        pad = jnp.zeros((page - n_step, n_kv * HEAD_DIM), F32)
        return jnp.concatenate([ref[s], pad], axis=0).astype(BF16)

    def page_pair(page_refs, s, m):
        return jnp.concatenate([page_2d(page_refs[s * n_pages + 2 * m]),
                                page_2d(page_refs[s * n_pages + 2 * m + 1])], axis=0)

    def score(q_bd, k2, mask):
        s = lax.dot_general(q_bd, k2, (((1,), (1,)), ((), ())), preferred_element_type=F32)
        return _sb_scores(s + bias, mask)

    step = lax.broadcasted_iota(jnp.int32, (rows, page), 0) % n_step
    key = lax.broadcasted_iota(jnp.int32, (rows, page), 1)
    pairs = list(reversed(range(n_pages // 2)))
    u1, u2 = _suffix_ones(page), _suffix_ones(2 * page)
    scored = []
    for s in range(n_sub):
        q = q_ref[s]
        q_bd = jnp.concatenate([jnp.where(row_kv == h, q, jnp.zeros_like(q)) for h in range(n_kv)], axis=1)
        scored.append([score(q_bd, new_tokens(kn_ref, s), key < step)]
                      + [score(q_bd, page_pair(kc_refs, s, m), None) for m in pairs])
    sums = [[_dot(sp, u1 if i == 0 else u2) for i, (_, sp) in enumerate(blocks)] for blocks in scored]
    for s in range(n_sub):
        values = [new_tokens(vn_ref, s)] + [page_pair(vc_refs, s, m) for m in pairs]
        acc = jnp.zeros((rows, n_kv * HEAD_DIM), F32)
        carry = jnp.zeros((rows, 1), F32)
        for (log_beta, sp), c_local, v2 in zip(scored[s], sums[s], values):
            c = c_local + carry
            acc = acc + _dot(_sb_weights(log_beta, c), v2)
            carry = c[:, :1] + sp[:, :1].astype(F32)
        o = jnp.zeros((rows, HEAD_DIM), F32)
        for h in range(n_kv):
            o = o + jnp.where(row_kv == h, acc[:, h * HEAD_DIM:(h + 1) * HEAD_DIM], 0.0)
        o_ref[s] = o.astype(o_ref.dtype)


def _page_index(seq, n_sub, p, b, pt):
    return (pt[b * n_sub + seq, p], 0, 0)


def _attn_sample(page_table, q, bias_col, cache_k, cache_v, k_new, v_new, *, n_sub):
    n_seq, rows, _ = q.shape
    n_pages = page_table.shape[1]
    _, page, n_kv, _ = cache_k.shape
    n_step = k_new.shape[1]
    assert page % SUBLANES == 0 and n_step <= page and n_pages % 2 == 0 and n_seq % n_sub == 0
    per_seq = lambda shape: pl.BlockSpec((n_sub,) + shape, lambda b, pt: (b,) + (0,) * len(shape))
    cache_k = cache_k.reshape(-1, page * n_kv, HEAD_DIM)
    cache_v = cache_v.reshape(-1, page * n_kv, HEAD_DIM)
    pages = [pl.BlockSpec((1, page * n_kv, HEAD_DIM), functools.partial(_page_index, s, n_sub, p))
             for s in range(n_sub) for p in range(n_pages)]
    grid_spec = pltpu.PrefetchScalarGridSpec(
        num_scalar_prefetch=1, grid=(n_seq // n_sub,),
        in_specs=[per_seq((rows, HEAD_DIM)), pl.BlockSpec((rows, 1), lambda b, pt: (0, 0)),
                  per_seq((n_step, n_kv * HEAD_DIM)), per_seq((n_step, n_kv * HEAD_DIM))] + pages + pages,
        out_specs=per_seq((rows, HEAD_DIM)))
    return pl.pallas_call(
        functools.partial(_attn_sample_kernel, n_step=n_step, n_pages=n_pages, n_sub=n_sub),
        grid_spec=grid_spec, out_shape=jax.ShapeDtypeStruct((n_seq, rows, HEAD_DIM), BF16),
        compiler_params=_cparams("parallel"), name="attn_sample",
    )(page_table, q, bias_col, k_new, v_new, *([cache_k] * len(pages)), *([cache_v] * len(pages)))


def kernel(x_prompt, x_sample, p_prompt, p_sample, state_ssm_re, state_ssm_im, cache_k, cache_v, page_table, s5_lam_re, s5_lam_im, s5_log_dt, s5_b_re, s5_b_im, s5_c_re, s5_c_im, s5_d, s5_w_gate, s5_w_out, w_k, w_v, sb_w_q, sb_w_o, sb_bias, ln_mix_g, ln_mix_b, mlp_w1, mlp_w2, ln_mlp_g, ln_mlp_b, ple_w_proj, ple_w_gate, ln_ple_g, ln_ple_b):
    _, t_p, dm = x_prompt.shape
    n_seq, n_step, _ = x_sample.shape
    n_grp_total, p_state = s5_lam_re.shape[1:]
    n_blk = dm // CH_BLOCK
    gpb = CH_BLOCK // SSM_GROUP
    ns = gpb * p_state
    n_heads = sb_w_q.shape[2] // HEAD_DIM
    rows_s = n_seq * n_step
    assert x_prompt.shape[0] == 1 and n_step == SUBLANES and n_seq % SUBLANES == 0
    bf = lambda a: a.astype(BF16)
    vec = lambda a: a.reshape(1, -1)

    def to_group_order(a):
        c = a.shape[-1]
        return a.reshape(n_seq // SUBLANES, SUBLANES, n_step, c).transpose(0, 2, 1, 3).reshape(rows_s, c)

    def to_batch_order(a):
        c = a.shape[-1]
        return a.reshape(n_seq // SUBLANES, n_step, SUBLANES, c).transpose(0, 2, 1, 3).reshape(n_seq, n_step, c)

    xp = x_prompt.reshape(t_p, dm)
    xs = to_group_order(x_sample)
    pp = [p_prompt[i].reshape(t_p, -1) for i in range(DEPTH)]
    ps = [to_group_order(p_sample[i]) for i in range(DEPTH)]

    w_gate, w_out, w_o = bf(s5_w_gate[0]), bf(s5_w_out[0]), bf(sb_w_o[0])
    wp, wg = bf(ple_w_proj), bf(ple_w_gate)
    wq, wk, wv = bf(sb_w_q[0]), bf(w_k), bf(w_v)

    def glu(g, x):
        return _glu(g, x, w_gate, w_out, vec(ln_mix_g[0]), vec(ln_mix_b[0]), tm=256)

    def attn_out(o, x):
        return _attn_out(o, x, w_o, vec(ln_mix_g[1]), vec(ln_mix_b[1]), tm=512)

    def mlp(x, i):
        return _mlp(x, mlp_w1, mlp_w2, i, vec(ln_mlp_g[i]), vec(ln_mlp_b[i]), tm=1024, tf=512)

    def ple(x, p, i):
        return _ple(x, p, wp, wg, i, vec(ln_ple_g[i]), vec(ln_ple_b[i]), tm=512)

    ab_re, ab_im, bb_re, bb_im = _s5_prep(s5_lam_re[0], s5_lam_im[0], s5_log_dt[0], s5_b_re[0], s5_b_im[0])
    wb = bf(jnp.concatenate([_block_diag_in(bb_re, n_blk, gpb, p_state),
                             _block_diag_in(bb_im, n_blk, gpb, p_state)], axis=-1))
    wc_re = bf(_block_diag_out(s5_c_re[0], n_blk, gpb, p_state))
    wc_im = bf(_block_diag_out(s5_c_im[0], n_blk, gpb, p_state))
    a_re = ab_re.reshape(n_blk, ns)
    a_im = ab_im.reshape(n_blk, ns)
    d = vec(s5_d[0])
    g_p, hp_re, hp_im = _scan_prompt(xp, wb, wc_re, wc_im, a_re, a_im, d, tb=128)
    g_s, hs_re, hs_im = _scan_sample(xs, wb, wc_re, wc_im, a_re, a_im, d,
                                     state_ssm_re[0].reshape(n_seq, -1), state_ssm_im[0].reshape(n_seq, -1),
                                     n_step=n_step)
    xp = ple(mlp(glu(g_p, xp), 0), pp[0], 0)
    xs = ple(mlp(glu(g_s, xs), 0), ps[0], 0)

    tq = 256
    q_per_kv = n_heads // N_KV_HEADS
    qt, k_p, v_p, kb, vt = _qkv_prompt(xp, wq.T, wk, wv, wv.T, tm=tq)
    bias_row = jnp.repeat(sb_bias[0].reshape(N_KV_HEADS, q_per_kv), tq, axis=1)
    o_p = _attn_prompt(bias_row.reshape(N_KV_HEADS, 1, q_per_kv * tq), qt, kb, vt, tq=tq, n_str=2)
    q_s, k_s, v_s = _qkv_sample(xs, wq, wk, wv, tm=256)
    k_s = to_batch_order(k_s)
    v_s = to_batch_order(v_s)
    q_s = to_batch_order(q_s).reshape(n_seq, n_step, n_heads, HEAD_DIM).transpose(0, 2, 1, 3)
    bias_col = jnp.repeat(sb_bias[0], n_step).reshape(n_heads * n_step, 1)
    o_s = _attn_sample(page_table, q_s.reshape(n_seq, n_heads * n_step, HEAD_DIM), bias_col,
                       cache_k, cache_v, k_s, v_s, n_sub=1)
    o_s = o_s.reshape(n_seq, n_heads, n_step, HEAD_DIM).transpose(0, 2, 1, 3).reshape(n_seq, n_step, -1)
    xp = ple(mlp(attn_out(o_p, xp), 1), pp[1], 1)
    xs = ple(mlp(attn_out(to_group_order(o_s), xs), 1), ps[1], 1)

    state = lambda a: a.reshape(1, -1, n_grp_total, p_state)
    kv = lambda a, lead: a.reshape(lead + (N_KV_HEADS, HEAD_DIM))
    return (xp.reshape(1, t_p, dm), to_batch_order(xs),
            state(hp_re), state(hp_im), kv(k_p, (1, t_p)), kv(v_p, (1, t_p)),
            state(hs_re), state(hs_im), kv(k_s, (n_seq, n_step)), kv(v_s, (n_seq, n_step)))
```
